```python
import math
import jax
import jax.numpy as jnp
from jax import lax
import numpy as np

D_MODEL = 1024
BATCH = 8
SEQ = 2048
DEPTH = 2
DEC_BATCH = 128
DEC_SEQ = 1
PAST_LEN = 16384
PAGE_SIZE = 128

A_HEADS = 8
A_NOPE = 64
A_ROPE = 32
A_VDIM = 64
A_Q_LORA = 256
A_KV_LORA = 128
A_Q_BLOCK = 128
ROPE_THETA = 10000.0
M_HEADS = 4
M_DH = 64
M_WIDTH = M_HEADS * M_DH
M_CONV = 4
M_CHUNK = 64
C_HEADS = 4
C_DH = 64
C_WIDTH = C_HEADS * C_DH
C_BLOCK = 256
C_TOPK = 3
C_Q_CHUNK = 64
REL_BUCKETS = 32
REL_MAX_DIST = 128
N_BRANCH = 3
D_FF = ((8 * D_MODEL + 3 * 256 - 1) // (3 * 256)) * 256
DEEPNORM_ALPHA = (2 * DEPTH) ** 0.25
DEEPNORM_BETA = (8 * DEPTH) ** -0.25
LN_EPS = 1e-5
RMS_EPS = 1e-6
IN_SIZES = (A_Q_LORA, A_KV_LORA, A_ROPE, M_WIDTH, M_WIDTH, M_HEADS, M_HEADS, C_WIDTH, C_WIDTH, C_WIDTH, N_BRANCH * D_MODEL)
IN_COLS = sum(IN_SIZES)

kernel_name = 'hybrid_mla_mlstm_moba_step'


def layer_norm(x, g, b):
    xf = x.astype(jnp.float32)
    mu = jnp.mean(xf, -1, keepdims=True)
    var = jnp.mean(jnp.square(xf - mu), -1, keepdims=True)
    y = (xf - mu) * lax.rsqrt(var + LN_EPS) * g.astype(jnp.float32) + b.astype(jnp.float32)
    return y.astype(x.dtype)


def rms_norm(x, g):
    xf = x.astype(jnp.float32)
    y = xf * lax.rsqrt(jnp.mean(jnp.square(xf), -1, keepdims=True) + RMS_EPS) * g.astype(jnp.float32)
    return y.astype(x.dtype)


def split_cols(z):
    idx, acc = [], 0
    for s in IN_SIZES[:-1]:
        acc += s
        idx.append(acc)
    return jnp.split(z, idx, axis=-1)


def rope_cos_sin(pos):
    inv = ROPE_THETA ** (-jnp.arange(0, A_ROPE, 2, dtype=jnp.float32) / A_ROPE)
    ang = pos.astype(jnp.float32)[:, None] * inv[None, :]
    return jnp.cos(ang), jnp.sin(ang)


def apply_rope(x, cos, sin):
    xf = x.astype(jnp.float32)
    x1, x2 = xf[..., : A_ROPE // 2], xf[..., A_ROPE // 2:]
    return jnp.concatenate([x1 * cos - x2 * sin, x1 * sin + x2 * cos], -1).astype(x.dtype)


def rel_bucket(dist):
    n = jnp.maximum(dist, 0)
    exact = REL_BUCKETS // 2
    large = exact + (jnp.log(jnp.maximum(n, 1).astype(jnp.float32) / exact)
                     / math.log(REL_MAX_DIST / exact) * (REL_BUCKETS - exact)).astype(jnp.int32)
    return jnp.where(n < exact, n, jnp.minimum(large, REL_BUCKETS - 1))


def ada_mod(c, w_ada, b_ada):
    a = jax.nn.silu(c) @ w_ada + b_ada
    return [t[:, None, :] for t in jnp.split(a, 6, axis=-1)]


def layer_in(x, c, w_ada, b_ada, w_in):
    mods = ada_mod(c, w_ada, b_ada)
    h = x * (1 + mods[1]) + mods[0]
    return split_cols(h @ w_in), mods


def layer_out(x, ya, ym, yc, g_raw, mods, w_br_a, w_br_m, w_br_c, w_out, ln1_g, ln1_b,
              w_ff_gate, w_ff_up, w_ff_down, ln2_g, ln2_b):
    sh1, sc1, g1, sh2, sc2, g2 = mods
    ga, gm, gc = jnp.split(jax.nn.sigmoid(g_raw), N_BRANCH, axis=-1)
    mix = (ga * (ya @ w_br_a) + gm * (ym @ w_br_m) + gc * (yc @ w_br_c)) @ w_out
    x = layer_norm(DEEPNORM_ALPHA * x + g1 * mix, ln1_g, ln1_b)
    h = x * (1 + sc2) + sh2
    f = (jax.nn.silu(h @ w_ff_gate) * (h @ w_ff_up)) @ w_ff_down
    return layer_norm(DEEPNORM_ALPHA * x + g2 * f, ln2_g, ln2_b)


def mla_project(cq, ckv, kr, pos, g_q, w_qb, g_kv):
    B, S, _ = cq.shape
    q = (rms_norm(cq, g_q) @ w_qb).reshape(B, S, A_HEADS, A_NOPE + A_ROPE)
    cos, sin = rope_cos_sin(pos)
    q_nope = q[..., :A_NOPE]
    q_rope = apply_rope(q[..., A_NOPE:], cos[:, None, :], sin[:, None, :])
    lat = rms_norm(ckv, g_kv)
    k_rope = apply_rope(kr, cos, sin)
    return q_nope, q_rope, lat, k_rope


def mla_prompt_attn(q_nope, q_rope, lat, k_rope, w_kvb):
    B, S = lat.shape[:2]
    w_uk, w_uv = w_kvb[..., :A_NOPE], w_kvb[..., A_NOPE:]
    k_nope = jnp.einsum('bsc,chn->bshn', lat, w_uk)
    v = jnp.einsum('bsc,chv->bshv', lat, w_uv)
    scale = (A_NOPE + A_ROPE) ** -0.5
    kpos = jnp.arange(S)
    nqb = S // A_Q_BLOCK

    def blk(args):
        qn, qr, qpos = args
        s = jnp.einsum('bqhn,bkhn->bhqk', qn, k_nope) + jnp.einsum('bqhr,bkr->bhqk', qr, k_rope)
        s = jnp.where(qpos[:, None] >= kpos[None, :], s.astype(jnp.float32) * scale, -jnp.inf)
        p = jax.nn.softmax(s, axis=-1).astype(v.dtype)
        return jnp.einsum('bhqk,bkhv->bqhv', p, v)

    qn_b = jnp.moveaxis(q_nope.reshape(B, nqb, A_Q_BLOCK, A_HEADS, A_NOPE), 1, 0)
    qr_b = jnp.moveaxis(q_rope.reshape(B, nqb, A_Q_BLOCK, A_HEADS, A_ROPE), 1, 0)
    qpos_b = jnp.arange(S).reshape(nqb, A_Q_BLOCK)
    o = lax.map(blk, (qn_b, qr_b, qpos_b))
    return jnp.moveaxis(o, 0, 1).reshape(B, S, A_HEADS * A_VDIM)


def mla_sample_attn(q_nope, q_rope, lat, k_rope, cache_lat, cache_kr, layer, page_table, w_kvb):
    DB, T = lat.shape[:2]
    w_uk, w_uv = w_kvb[..., :A_NOPE], w_kvb[..., A_NOPE:]
    lat_past = cache_lat[layer, page_table].reshape(DB, -1, A_KV_LORA)
    kr_past = cache_kr[layer, page_table].reshape(DB, -1, A_ROPE)
    P = lat_past.shape[1]
    scale = (A_NOPE + A_ROPE) ** -0.5
    q_abs = jnp.einsum('bthn,chn->bthc', q_nope, w_uk)
    s_past = (jnp.einsum('bthc,bsc->bhts', q_abs, lat_past)
              + jnp.einsum('bthr,bsr->bhts', q_rope, kr_past)).astype(jnp.float32) * scale
    s_new = (jnp.einsum('bthc,buc->bhtu', q_abs, lat)
             + jnp.einsum('bthr,bur->bhtu', q_rope, k_rope)).astype(jnp.float32) * scale
    s_new = jnp.where(jnp.tril(jnp.ones((T, T), bool)), s_new, -jnp.inf)
    p = jax.nn.softmax(jnp.concatenate([s_past, s_new], -1), axis=-1).astype(lat.dtype)
    o_lat = jnp.einsum('bhts,bsc->bthc', p[..., :P], lat_past) + jnp.einsum('bhtu,buc->bthc', p[..., P:], lat)
    return jnp.einsum('bthc,chv->bthv', o_lat, w_uv).reshape(DB, T, A_HEADS * A_VDIM)


def causal_dwconv(u_ext, w, b):
    y = lax.conv_general_dilated(u_ext, w[:, None, :], window_strides=(1,), padding='VALID',
                                 dimension_numbers=('NWC', 'WIO', 'NWC'), feature_group_count=M_WIDTH)
    return y + b


def mlstm_recurrence(q, k, v, i_pre, log_f, C0, n0, m0):
    B, H, S, D = q.shape
    L = math.gcd(S, M_CHUNK)
    NC = S // L
    f32 = jnp.float32

    def chunks(a):
        return jnp.moveaxis(a.astype(f32).reshape(B, H, NC, L, *a.shape[3:]), 2, 0)

    causal = jnp.tril(jnp.ones((L, L), bool))

    def step(carry, inp):
        C, n, m = carry
        qx, kx, vx, ix, fx = inp
        b = jnp.cumsum(fx, axis=-1)
        a = b + m[..., None]
        dm = jnp.where(causal, b[..., :, None] - b[..., None, :] + ix[..., None, :], -jnp.inf)
        mt = jnp.maximum(a, jnp.max(dm, -1))
        w_inter = jnp.exp(a - mt)
        s = jnp.einsum('bhtd,bhsd->bhts', qx, kx) * jnp.exp(dm - mt[..., None])
        num = w_inter[..., None] * jnp.einsum('bhtd,bhde->bhte', qx, C) + jnp.einsum('bhts,bhse->bhte', s, vx)
        den = w_inter * jnp.einsum('bhtd,bhd->bht', qx, n) + jnp.sum(s, -1)
        h = num / jnp.maximum(jnp.abs(den), jnp.exp(-mt))[..., None]
        b_end = b[..., -1]
        g = b_end[..., None] - b + ix
        m_new = jnp.maximum(b_end + m, jnp.max(g, -1))
        w_old = jnp.exp(b_end + m - m_new)
        w_tok = jnp.exp(g - m_new[..., None])
        C_new = w_old[..., None, None] * C + jnp.einsum('bhs,bhsd,bhse->bhde', w_tok, kx, vx)
        n_new = w_old[..., None] * n + jnp.einsum('bhs,bhsd->bhd', w_tok, kx)
        return (C_new, n_new, m_new), h

    init = (C0.astype(f32), n0.astype(f32), m0.astype(f32))
    (C, n, m), hs = lax.scan(step, init, (chunks(q), chunks(k), chunks(v), chunks(i_pre), chunks(log_f)))
    h = jnp.moveaxis(hs, 0, 2).reshape(B, H, S, D)
    return h.astype(q.dtype), C.astype(C0.dtype), n.astype(n0.dtype), m.astype(m0.dtype)


def mlstm_branch(u_ext, u, o_raw, i_raw, f_raw, C0, n0, m0, conv_w, conv_b, w_mq, w_mk, w_mv, b_i, b_f):
    B, S, _ = u.shape
    uc = jax.nn.silu(causal_dwconv(u_ext, conv_w, conv_b)).reshape(B, S, M_HEADS, M_DH)
    q = jnp.einsum('bshd,hde->bhse', uc, w_mq)
    k = jnp.einsum('bshd,hde->bhse', uc, w_mk) * (M_DH ** -0.5)
    v = jnp.einsum('bshd,hde->bhse', u.reshape(B, S, M_HEADS, M_DH), w_mv)
    i_pre = jnp.moveaxis((i_raw + b_i).astype(jnp.float32), -1, 1)
    log_f = jnp.moveaxis(jax.nn.log_sigmoid((f_raw + b_f).astype(jnp.float32)), -1, 1)
    h, C, n, m = mlstm_recurrence(q, k, v, i_pre, log_f, C0, n0, m0)
    h = jnp.moveaxis(h, 1, 2).reshape(B, S, M_WIDTH) * jax.nn.sigmoid(o_raw)
    return h, C, n, m


def moba_prompt_attn(q, k, v, rel_table):
    B, S = q.shape[:2]
    f32 = jnp.float32
    nb = -(-S // C_BLOCK)
    pad = nb * C_BLOCK - S
    k_blk = jnp.moveaxis(jnp.pad(k, ((0, 0), (0, pad), (0, 0), (0, 0))).reshape(B, nb, C_BLOCK, C_HEADS, C_DH), 3, 1)
    v_blk = jnp.moveaxis(jnp.pad(v, ((0, 0), (0, pad), (0, 0), (0, 0))).reshape(B, nb, C_BLOCK, C_HEADS, C_DH), 3, 1)
    k_mean = jnp.mean(k_blk.astype(f32), axis=3)
    k3 = min(C_TOPK, nb - 1)
    scale = C_DH ** -0.5
    table_t = rel_table.T
    bi = jnp.arange(B)[:, None, None, None]
    hi = jnp.arange(C_HEADS)[None, :, None, None]

    def chunk(args):
        qc, c = args
        qh = jnp.moveaxis(qc, 2, 1)
        q0 = c * C_Q_CHUNK
        qpos = q0 + jnp.arange(C_Q_CHUNK)
        ob = q0 // C_BLOCK
        own_k = lax.dynamic_index_in_dim(k_blk, ob, axis=2, keepdims=False)
        own_v = lax.dynamic_index_in_dim(v_blk, ob, axis=2, keepdims=False)
        d_own = qpos[:, None] - (ob * C_BLOCK + jnp.arange(C_BLOCK))[None, :]
        s_own = jnp.einsum('bhqd,bhkd->bhqk', qh, own_k).astype(f32) * scale + table_t[:, rel_bucket(d_own)]
        s_own = jnp.where(d_own >= 0, s_own, -jnp.inf)
        if k3 == 0:
            p = jax.nn.softmax(s_own, axis=-1).astype(v.dtype)
            return jnp.einsum('bhqk,bhkd->bqhd', p, own_v)
        gs = jnp.einsum('bhqd,bhjd->bhqj', qh.astype(f32), k_mean)
        gs = jnp.where(jnp.arange(nb) < ob, gs, -jnp.inf)
        _, idx = lax.top_k(gs, k3)
        sel_k = k_blk[bi, hi, idx]
        sel_v = v_blk[bi, hi, idx]
        kpos_sel = idx[..., None] * C_BLOCK + jnp.arange(C_BLOCK)
        bias_sel = table_t[hi[..., None], rel_bucket(qpos[:, None, None] - kpos_sel)]
        s_sel = jnp.einsum('bhqd,bhqjkd->bhqjk', qh, sel_k).astype(f32) * scale + bias_sel
        s_sel = jnp.where((jnp.arange(k3) < ob)[:, None], s_sel, -jnp.inf).reshape(B, C_HEADS, C_Q_CHUNK, k3 * C_BLOCK)
        p = jax.nn.softmax(jnp.concatenate([s_sel, s_own], -1), axis=-1).astype(v.dtype)
        p_sel = p[..., :k3 * C_BLOCK].reshape(B, C_HEADS, C_Q_CHUNK, k3, C_BLOCK)
        return (jnp.einsum('bhqjk,bhqjkd->bqhd', p_sel, sel_v)
                + jnp.einsum('bhqk,bhkd->bqhd', p[..., k3 * C_BLOCK:], own_v))

    nqc = S // C_Q_CHUNK
    q_ch = jnp.moveaxis(q.reshape(B, nqc, C_Q_CHUNK, C_HEADS, C_DH), 1, 0)
    o = lax.map(chunk, (q_ch, jnp.arange(nqc)))
    return jnp.moveaxis(o, 0, 1).reshape(B, S, C_WIDTH)


def moba_sample_attn(q, k, v, cache_k, cache_v, page_sum, layer, page_table, rel_table):
    DB, T = q.shape[:2]
    f32 = jnp.float32
    n_pages = page_table.shape[1]
    past = n_pages * PAGE_SIZE
    ppb = C_BLOCK // PAGE_SIZE
    nbp = past // C_BLOCK
    k3 = min(C_TOPK, nbp)
    scale = C_DH ** -0.5
    table_t = rel_table.T
    qh = jnp.moveaxis(q, 2, 1)
    qpos = past + jnp.arange(T)
    own_k, own_v, kpos_own = jnp.moveaxis(k, 2, 1), jnp.moveaxis(v, 2, 1), qpos
    own_pages = page_table[:, nbp * ppb:]
    if own_pages.shape[1] > 0:
        pk = jnp.moveaxis(cache_k[layer, own_pages].reshape(DB, -1, C_HEADS, C_DH), 2, 1)
        pv = jnp.moveaxis(cache_v[layer, own_pages].reshape(DB, -1, C_HEADS, C_DH), 2, 1)
        kpos_own = jnp.concatenate([nbp * C_BLOCK + jnp.arange(pk.shape[2]), qpos])
        own_k = jnp.concatenate([pk, own_k], axis=2)
        own_v = jnp.concatenate([pv, own_v], axis=2)
    d_own = qpos[:, None] - kpos_own[None, :]
    s_own = jnp.einsum('bhtd,bhkd->bhtk', qh, own_k).astype(f32) * scale + table_t[:, rel_bucket(d_own)]
    s_own = jnp.where(d_own >= 0, s_own, -jnp.inf)
    if k3 == 0:
        p = jax.nn.softmax(s_own, axis=-1).astype(v.dtype)
        return jnp.einsum('bhtk,bhkd->bthd', p, own_v).reshape(DB, T, C_WIDTH)
    k_mean = page_sum[layer, page_table[:, :nbp * ppb]].reshape(DB, nbp, ppb, C_HEADS, C_DH).sum(2) / C_BLOCK
    gs = jnp.einsum('bhtd,bjhd->bhtj', qh.astype(f32), k_mean)
    _, idx = lax.top_k(gs, k3)
    logical = idx[..., None] * ppb + jnp.arange(ppb)
    phys = page_table[jnp.arange(DB)[:, None, None, None, None], logical]
    hi = jnp.arange(C_HEADS)[None, :, None, None, None]
    sel_k = cache_k[layer, phys, :, hi, :].reshape(DB, C_HEADS, T, k3, C_BLOCK, C_DH)
    sel_v = cache_v[layer, phys, :, hi, :].reshape(DB, C_HEADS, T, k3, C_BLOCK, C_DH)
    kpos_sel = idx[..., None] * C_BLOCK + jnp.arange(C_BLOCK)
    bias_sel = table_t[hi, rel_bucket(qpos[:, None, None] - kpos_sel)]
    s_sel = (jnp.einsum('bhtd,bhtjkd->bhtjk', qh, sel_k).astype(f32) * scale + bias_sel).reshape(DB, C_HEADS, T, k3 * C_BLOCK)
    p = jax.nn.softmax(jnp.concatenate([s_sel, s_own], -1), axis=-1).astype(v.dtype)
    p_sel = p[..., :k3 * C_BLOCK].reshape(DB, C_HEADS, T, k3, C_BLOCK)
    o = jnp.einsum('bhtjk,bhtjkd->bthd', p_sel, sel_v) + jnp.einsum('bhtk,bhkd->bthd', p[..., k3 * C_BLOCK:], own_v)
    return o.reshape(DB, T, C_WIDTH)


def setup_inputs(seed: int = 0) -> dict:
    key = jax.random.key(seed)
    keys = iter(jax.random.split(key, 48))

    def nrm(shape, scale=1.0):
        x = jax.random.normal(next(keys), shape, jnp.float32)
        return x if scale == 1.0 else x * scale

    n_pages = PAST_LEN // PAGE_SIZE
    n_used = DEC_BATCH * n_pages
    n_pool = n_used + max(1, n_used // 4)
    page_table = jax.random.permutation(next(keys), n_pool)[:n_used].reshape(DEC_BATCH, n_pages).astype(jnp.int32)
    d = D_MODEL
    return {
        'x_prompt': nrm((BATCH, SEQ, d)),
        'x_sample': nrm((DEC_BATCH, DEC_SEQ, d)),
        'cache_mla_latent': nrm((DEPTH, n_pool, PAGE_SIZE, A_KV_LORA)),
        'cache_mla_krope': nrm((DEPTH, n_pool, PAGE_SIZE, A_ROPE)),
        'cache_moba_k': nrm((DEPTH, n_pool, PAGE_SIZE, C_HEADS, C_DH)),
        'cache_moba_v': nrm((DEPTH, n_pool, PAGE_SIZE, C_HEADS, C_DH)),
        'state_mlstm_C': nrm((DEPTH, DEC_BATCH, M_HEADS, M_DH, M_DH), 0.3),
        'state_mlstm_n': nrm((DEPTH, DEC_BATCH, M_HEADS, M_DH), 0.3),
        'state_mlstm_m': nrm((DEPTH, DEC_BATCH, M_HEADS), 0.5),
        'state_mlstm_conv': nrm((DEPTH, DEC_BATCH, M_CONV - 1, M_WIDTH)),
        'page_table': page_table,
        'c_prompt': nrm((BATCH, d)),
        'c_sample': nrm((DEC_BATCH, d)),
        'rel_table': nrm((REL_BUCKETS, C_HEADS), 0.5),
        'w_ada': nrm((DEPTH, d, 6 * d), d ** -0.5),
        'b_ada': nrm((DEPTH, 6 * d), 0.02),
        'w_in': nrm((DEPTH, d, IN_COLS), d ** -0.5),
        'g_q_norm': 1.0 + nrm((DEPTH, A_Q_LORA), 0.02),
        'w_q_up': nrm((DEPTH, A_Q_LORA, A_HEADS * (A_NOPE + A_ROPE)), A_Q_LORA ** -0.5),
        'g_kv_norm': 1.0 + nrm((DEPTH, A_KV_LORA), 0.02),
        'w_kv_up': nrm((DEPTH, A_KV_LORA, A_HEADS, A_NOPE + A_VDIM), A_KV_LORA ** -0.5),
        'conv_w': nrm((DEPTH, M_CONV, M_WIDTH), M_CONV ** -0.5),
        'conv_b': nrm((DEPTH, M_WIDTH), 0.02),
        'w_mq': nrm((DEPTH, M_HEADS, M_DH, M_DH), M_DH ** -0.5),
        'w_mk': nrm((DEPTH, M_HEADS, M_DH, M_DH), M_DH ** -0.5),
        'w_mv': nrm((DEPTH, M_HEADS, M_DH, M_DH), M_DH ** -0.5),
        'b_i': nrm((DEPTH, M_HEADS), 0.1),
        'b_f': 3.0 + nrm((DEPTH, M_HEADS), 0.1),
        'w_br_a': nrm((DEPTH, A_HEADS * A_VDIM, d), (A_HEADS * A_VDIM) ** -0.5),
        'w_br_m': nrm((DEPTH, M_WIDTH, d), M_WIDTH ** -0.5),
        'w_br_c': nrm((DEPTH, C_WIDTH, d), C_WIDTH ** -0.5),
        'w_out': nrm((DEPTH, d, d), d ** -0.5 * DEEPNORM_BETA),
        'ln1_g': 1.0 + nrm((DEPTH, d), 0.02),
        'ln1_b': nrm((DEPTH, d), 0.02),
        'w_ff_gate': nrm((DEPTH, d, D_FF), d ** -0.5),
        'w_ff_up': nrm((DEPTH, d, D_FF), d ** -0.5),
        'w_ff_down': nrm((DEPTH, D_FF, d), D_FF ** -0.5 * DEEPNORM_BETA),
        'ln2_g': 1.0 + nrm((DEPTH, d), 0.02),
        'ln2_b': nrm((DEPTH, d), 0.02),
    }


def reference(x_prompt, x_sample, cache_mla_latent, cache_mla_krope, cache_moba_k, cache_moba_v,
              state_mlstm_C, state_mlstm_n, state_mlstm_m, state_mlstm_conv, page_table,
              c_prompt, c_sample, rel_table, w_ada, b_ada, w_in, g_q_norm, w_q_up, g_kv_norm, w_kv_up,
              conv_w, conv_b, w_mq, w_mk, w_mv, b_i, b_f, w_br_a, w_br_m, w_br_c, w_out,
              ln1_g, ln1_b, w_ff_gate, w_ff_up, w_ff_down, ln2_g, ln2_b):
    B, S, _ = x_prompt.shape
    DB, T, _ = x_sample.shape
    past = page_table.shape[1] * PAGE_SIZE
    pos_p = jnp.arange(S, dtype=jnp.int32)
    pos_s = past + jnp.arange(T, dtype=jnp.int32)
    page_sum = jnp.sum(cache_moba_k, axis=2, dtype=jnp.float32) if past // C_BLOCK > 0 else None
    xp, xs = x_prompt, x_sample
    new_p = [[] for _ in range(8)]
    new_s = [[] for _ in range(8)]
    for l in range(DEPTH):
        out_w = (w_br_a[l], w_br_m[l], w_br_c[l], w_out[l], ln1_g[l], ln1_b[l],
                 w_ff_gate[l], w_ff_up[l], w_ff_down[l], ln2_g[l], ln2_b[l])
        mlstm_w = (conv_w[l], conv_b[l], w_mq[l], w_mk[l], w_mv[l], b_i[l], b_f[l])

        (cq, ckv, kr, u, o_raw, i_raw, f_raw, qc, kc, vc, g_raw), mods = layer_in(xp, c_prompt, w_ada[l], b_ada[l], w_in[l])
        qn, qr, lat, krope = mla_project(cq, ckv, kr, pos_p, g_q_norm[l], w_q_up[l], g_kv_norm[l])
        ya = mla_prompt_attn(qn, qr, lat, krope, w_kv_up[l])
        u_ext = jnp.pad(u, ((0, 0), (M_CONV - 1, 0), (0, 0)))
        C0 = jnp.zeros((B, M_HEADS, M_DH, M_DH), xp.dtype)
        n0 = jnp.zeros((B, M_HEADS, M_DH), xp.dtype)
        m0 = jnp.zeros((B, M_HEADS), xp.dtype)
        ym, C1, n1, m1 = mlstm_branch(u_ext, u, o_raw, i_raw, f_raw, C0, n0, m0, *mlstm_w)
        kh = kc.reshape(B, S, C_HEADS, C_DH)
        vh = vc.reshape(B, S, C_HEADS, C_DH)
        yc = moba_prompt_attn(qc.reshape(B, S, C_HEADS, C_DH), kh, vh, rel_table)
        xp = layer_out(xp, ya, ym, yc, g_raw, mods, *out_w)
        for lst, val in zip(new_p, (lat, krope, kh, vh, C1, n1, m1, u_ext[:, -(M_CONV - 1):])):
            lst.append(val)

        (cq, ckv, kr, u, o_raw, i_raw, f_raw, qc, kc, vc, g_raw), mods = layer_in(xs, c_sample, w_ada[l], b_ada[l], w_in[l])
        qn, qr, lat, krope = mla_project(cq, ckv, kr, pos_s, g_q_norm[l], w_q_up[l], g_kv_norm[l])
        ya = mla_sample_attn(qn, qr, lat, krope, cache_mla_latent, cache_mla_krope, l, page_table, w_kv_up[l])
        u_ext = jnp.concatenate([state_mlstm_conv[l].astype(u.dtype), u], axis=1)
        ym, C1, n1, m1 = mlstm_branch(u_ext, u, o_raw, i_raw, f_raw, state_mlstm_C[l], state_mlstm_n[l],
                                      state_mlstm_m[l], *mlstm_w)
        kh = kc.reshape(DB, T, C_HEADS, C_DH)
        vh = vc.reshape(DB, T, C_HEADS, C_DH)
        yc = moba_sample_attn(qc.reshape(DB, T, C_HEADS, C_DH), kh, vh, cache_moba_k, cache_moba_v,
                              page_sum, l, page_table, rel_table)
        xs = layer_out(xs, ya, ym, yc, g_raw, mods, *out_w)
        for lst, val in zip(new_s, (lat, krope, kh, vh, C1, n1, m1, u_ext[:, -(M_CONV - 1):])):
            lst.append(val)

    p_lat, p_kr, p_k, p_v, p_C, p_n, p_m, p_conv = [jnp.stack(a) for a in new_p]
    s_lat, s_kr, s_k, s_v, s_C, s_n, s_m, s_conv = [jnp.stack(a) for a in new_s]
    return (xp, xs, p_lat, p_kr, p_k, p_v, p_C, p_n, p_m, p_conv,
            s_lat, s_kr, s_k, s_v, s_C, s_n, s_m, s_conv)
```

```python
import functools
import math

import jax
import jax.numpy as jnp
from jax import lax
from jax.experimental import pallas as pl
from jax.experimental.pallas import tpu as pltpu

f32 = jnp.float32
bf16 = jnp.bfloat16
HI = lax.Precision.HIGHEST

D_MODEL = 1024
PAGE_SIZE = 128
A_HEADS = 8
A_NOPE = 64
A_ROPE = 32
A_VDIM = 64
A_Q_LORA = 256
A_KV_LORA = 128
ROPE_THETA = 10000.0
M_HEADS = 4
M_DH = 64
M_WIDTH = M_HEADS * M_DH
M_CONV = 4
C_HEADS = 4
C_DH = 64
C_WIDTH = C_HEADS * C_DH
C_BLOCK = 256
C_TOPK = 3
REL_BUCKETS = 32
REL_MAX_DIST = 128
LN_EPS = 1e-5
RMS_EPS = 1e-6
LANES = 128
SUBLANES = 8
VMEM_LIMIT = 56 * 1024 * 1024
NEG = -1e30

ZA_W = A_Q_LORA + A_KV_LORA + 2 * LANES
ZM_W = 2 * M_WIDTH + 2 * LANES
ZC_W = 3 * C_WIDTH
ZG_W = 3 * D_MODEL
PACK_W = ZA_W + ZM_W + ZC_W + ZG_W
HEAD_BLK = LANES


def _rel_thresholds():
    exact = REL_BUCKETS // 2
    out = []
    for k in range(1, REL_BUCKETS - exact):
        n = exact
        while int(math.log(n / exact) / math.log(REL_MAX_DIST / exact) * (REL_BUCKETS - exact)) < k:
            n += 1
        out.append(n)
    return tuple(out)


REL_THRESH = _rel_thresholds()


def _nt(a, b, precision=None):
    return lax.dot_general(a, b, (((1,), (1,)), ((), ())), precision=precision, preferred_element_type=f32)


def _tn(a, b):
    return lax.dot_general(a, b, (((0,), (0,)), ((), ())), preferred_element_type=f32)


def _dot(a, b, precision=None):
    return jnp.dot(a, b, precision=precision, preferred_element_type=f32)


def _silu(x):
    return x * jax.nn.sigmoid(x)


def _layer_norm(x, g, b):
    mu = jnp.mean(x, -1, keepdims=True)
    xc = x - mu
    var = jnp.mean(xc * xc, -1, keepdims=True)
    return xc * lax.rsqrt(var + LN_EPS) * g + b


def _rms_norm(x, g):
    return x * lax.rsqrt(jnp.mean(x * x, -1, keepdims=True) + RMS_EPS) * g


def _rel_bias(dist, tab_ref, h):
    n = jnp.maximum(dist, 0)
    exact = REL_BUCKETS // 2
    large = jnp.full(n.shape, exact, jnp.int32)
    for t in REL_THRESH:
        large = large + (n >= t).astype(jnp.int32)
    bucket = jnp.where(n < exact, n, large)
    val = jnp.full(n.shape, tab_ref[h * REL_BUCKETS + REL_BUCKETS - 1], f32)
    for b in range(REL_BUCKETS - 1):
        val = jnp.where(bucket == b, tab_ref[h * REL_BUCKETS + b], val)
    return val


def _params(sem):
    return pltpu.CompilerParams(dimension_semantics=sem, vmem_limit_bytes=VMEM_LIMIT)


def _ada_kernel(c_ref, w_ref, b_ref, o_ref):
    c = c_ref[...]
    o_ref[...] = _dot(_silu(c), w_ref[...], HI) + b_ref[...]


def ada_mods(c_all, w_ada, b_ada):
    depth, d, n = w_ada.shape
    rows = c_all.shape[0]
    tn = n // 4
    return pl.pallas_call(
        _ada_kernel,
        grid=(depth, n // tn),
        in_specs=[
            pl.BlockSpec((rows, d), lambda l, j: (0, 0)),
            pl.BlockSpec((None, d, tn), lambda l, j: (l, 0, j)),
            pl.BlockSpec((None, 1, tn), lambda l, j: (l, 0, j)),
        ],
        out_specs=pl.BlockSpec((None, rows, tn), lambda l, j: (l, 0, j)),
        out_shape=jax.ShapeDtypeStruct((depth, rows, n), f32),
        compiler_params=_params(("arbitrary", "arbitrary")),
        name="ada_mods",
    )(c_all, w_ada, b_ada.reshape(depth, 1, n))


def _mod_spec(mods, tm, col):
    per_row = mods.shape[1] != 1
    rows = tm if per_row else 1
    return pl.BlockSpec((1, rows, D_MODEL), lambda b, i, *_: (b, i if per_row else 0, col))


def _in_proj_kernel(x_ref, sh_ref, sc_ref, w_ref, za_ref, zm_ref, q_ref, k_ref, v_ref, sg_ref):
    h = (x_ref[0] * (1.0 + sc_ref[0]) + sh_ref[0]).astype(w_ref.dtype)

    def mm(c0, c1):
        return _dot(h, w_ref[:, c0:c1])

    c = 0
    za_ref[0] = mm(c, c + ZA_W)
    c += ZA_W
    zm_ref[0] = mm(c, c + ZM_W)
    c += ZM_W
    q_ref[0] = mm(c, c + C_WIDTH)
    k_ref[0] = mm(c + C_WIDTH, c + 2 * C_WIDTH)
    v_ref[0] = mm(c + 2 * C_WIDTH, c + 3 * C_WIDTH)
    c += ZC_W
    sg_ref[0] = jax.nn.sigmoid(mm(c, c + ZG_W)).astype(sg_ref.dtype)


def in_proj(x, mods, w_pack, tm):
    B, S, d = x.shape
    widths = (ZA_W, ZM_W, C_WIDTH, C_WIDTH, C_WIDTH, ZG_W)
    dts = (f32, f32, f32, f32, f32, bf16)
    return pl.pallas_call(
        _in_proj_kernel,
        grid=(B, S // tm),
        in_specs=[
            pl.BlockSpec((1, tm, d), lambda b, i: (b, i, 0)),
            _mod_spec(mods, tm, 0),
            _mod_spec(mods, tm, 1),
            pl.BlockSpec((d, PACK_W), lambda b, i: (0, 0)),
        ],
        out_specs=[pl.BlockSpec((1, tm, w), lambda b, i: (b, i, 0)) for w in widths],
        out_shape=[jax.ShapeDtypeStruct((B, S, w), t) for w, t in zip(widths, dts)],
        compiler_params=_params(("arbitrary", "arbitrary")),
        name="in_proj",
    )(x, mods, mods, w_pack)


def _mla_prep_kernel(za_ref, ct_ref, st_ref, gq_ref, gkv_ref, wq1_ref, wq2_ref, wuk_ref, wuv_ref,
                     qh_ref, kh_ref, vh_ref, lat_ref, kr_ref, *, prec):
    za = za_ref[0]
    cq = za[:, :A_Q_LORA]
    ckv = za[:, A_Q_LORA:A_Q_LORA + A_KV_LORA]
    krp = za[:, A_Q_LORA + A_KV_LORA:A_Q_LORA + A_KV_LORA + LANES]
    krs = za[:, A_Q_LORA + A_KV_LORA + LANES:]
    ct = ct_ref[...]
    st = st_ref[...]
    scale = (A_NOPE + A_ROPE) ** -0.5
    cqn = _rms_norm(cq, gq_ref[...]).astype(wq1_ref.dtype)
    q1 = _dot(cqn, wq1_ref[...], prec)
    q2 = _dot(cqn, wq2_ref[...], prec)
    lat = _rms_norm(ckv, gkv_ref[...])
    lat_ref[0] = lat
    kro = krp * ct + krs * st
    kr_ref[0] = kro[:, A_NOPE:A_NOPE + A_ROPE]
    latc = lat.astype(wuk_ref.dtype)
    kn = _dot(latc, wuk_ref[...], prec)
    for h in range(A_HEADS):
        blk = slice(h * HEAD_BLK, (h + 1) * HEAD_BLK)
        qh_ref[0, :, blk] = ((q1[:, blk] * ct + q2[:, blk] * st) * scale).astype(qh_ref.dtype)
        kh_ref[0, :, blk] = (kn[:, blk] + kro).astype(kh_ref.dtype)
    vh_ref[0] = _dot(latc, wuv_ref[...], prec).astype(vh_ref.dtype)


def mla_prep(za, ctab, stab, gq, gkv, wq1, wq2, wuk, wuv, tm, qk_dtype):
    B, S, _ = za.shape
    full = lambda a: pl.BlockSpec(a.shape, lambda b, i: (0,) * a.ndim)
    tok = lambda w: pl.BlockSpec((1, tm, w), lambda b, i: (b, i, 0))
    widths = (A_HEADS * HEAD_BLK, A_HEADS * HEAD_BLK, A_HEADS * A_VDIM, A_KV_LORA, A_ROPE)
    dts = (qk_dtype, qk_dtype, qk_dtype, f32, f32)
    prec = HI if qk_dtype == f32 else None
    return pl.pallas_call(
        functools.partial(_mla_prep_kernel, prec=prec),
        grid=(B, S // tm),
        in_specs=[tok(ZA_W),
                  pl.BlockSpec((tm, LANES), lambda b, i: (i, 0)),
                  pl.BlockSpec((tm, LANES), lambda b, i: (i, 0)),
                  full(gq), full(gkv), full(wq1), full(wq2), full(wuk), full(wuv)],
        out_specs=[tok(w) for w in widths],
        out_shape=[jax.ShapeDtypeStruct((B, S, w), t) for w, t in zip(widths, dts)],
        compiler_params=_params(("arbitrary", "arbitrary")),
        name="mla_prep",
    )(za, ctab, stab, gq, gkv, wq1, wq2, wuk, wuv)


def _mla_attn_kernel(q_ref, k_ref, v_ref, o_ref, *, tq):
    S = q_ref.shape[1]
    lane = lax.broadcasted_iota(jnp.int32, (tq, 2 * A_VDIM), 1)
    for t in range(S // tq):
        kv_len = (t + 1) * tq
        row = t * tq + lax.broadcasted_iota(jnp.int32, (tq, kv_len), 0)
        col = lax.broadcasted_iota(jnp.int32, (tq, kv_len), 1)
        causal = col <= row
        vpair = v_ref[0, :kv_len, :]
        outs = []
        for hh in range(2):
            blk = slice(hh * HEAD_BLK, (hh + 1) * HEAD_BLK)
            s = _nt(q_ref[0, t * tq:(t + 1) * tq, blk], k_ref[0, :kv_len, blk])
            s = jnp.where(causal, s, -jnp.inf)
            m = jnp.max(s, -1, keepdims=True)
            p = jnp.exp(s - m)
            l = jnp.sum(p, -1, keepdims=True)
            outs.append(_dot(p.astype(vpair.dtype), vpair) / l)
        o_ref[0, t * tq:(t + 1) * tq, :] = jnp.where(lane < A_VDIM, outs[0], outs[1]).astype(o_ref.dtype)


def mla_attn(qh, kh, vh, tq):
    B, S, _ = qh.shape
    pairs = A_HEADS // 2
    return pl.pallas_call(
        functools.partial(_mla_attn_kernel, tq=tq),
        grid=(B, pairs),
        in_specs=[pl.BlockSpec((1, S, 2 * HEAD_BLK), lambda b, p: (b, 0, p)),
                  pl.BlockSpec((1, S, 2 * HEAD_BLK), lambda b, p: (b, 0, p)),
                  pl.BlockSpec((1, S, 2 * A_VDIM), lambda b, p: (b, 0, p))],
        out_specs=pl.BlockSpec((1, S, 2 * A_VDIM), lambda b, p: (b, 0, p)),
        out_shape=jax.ShapeDtypeStruct((B, S, A_HEADS * A_VDIM), bf16),
        compiler_params=_params(("arbitrary", "arbitrary")),
        name="mla_attn",
    )(qh, kh, vh)


def _log_sigmoid(x):
    return jnp.minimum(x, 0.0) - jnp.log1p(jnp.exp(-jnp.abs(x)))


def _mlstm_kernel(zm_ref, cw_ref, cb_ref, wq_ref, wk_ref, wkt_ref, wv_ref, bi_ref, bf_ref,
                  ym_ref, c_out_ref, n_out_ref, m_out_ref,
                  c_st, n_st, m_row, m_col, tail):
    L = zm_ref.shape[1]
    W = M_WIDTH
    c = pl.program_id(1)

    @pl.when(c == 0)
    def _():
        c_st[...] = jnp.zeros_like(c_st)
        n_st[...] = jnp.zeros_like(n_st)
        m_row[...] = jnp.zeros_like(m_row)
        m_col[...] = jnp.zeros_like(m_col)
        tail[...] = jnp.zeros_like(tail)

    zm = zm_ref[0]
    u = zm[:, :W]
    o_raw = zm[:, W:2 * W]
    ig = zm[:, 2 * W:2 * W + LANES]
    fg = zm[:, 2 * W + LANES:]

    rows = lax.broadcasted_iota(jnp.int32, (L, W), 0)
    tl = tail[...]
    acc = u * cw_ref[M_CONV - 1:M_CONV, :]
    for j in range(1, M_CONV):
        ru = pltpu.roll(u, j, 0)
        rt = jnp.concatenate([pltpu.roll(tl, j, 0), jnp.zeros((L - SUBLANES, W), f32)], axis=0)
        acc = acc + jnp.where(rows < j, rt, ru) * cw_ref[M_CONV - 1 - j:M_CONV - j, :]
    tail[...] = u[L - SUBLANES:, :]
    uc = _silu(acc + cb_ref[...]).astype(bf16)
    ub = u.astype(bf16)

    q = _dot(uc, wq_ref[...])
    k = _dot(uc, wk_ref[...])
    kt = _nt(wkt_ref[...], uc)
    v = _dot(ub, wv_ref[...])
    qb = q.astype(bf16)
    kb = k.astype(bf16)
    vb = v.astype(bf16)

    i_c = ig + bi_ref[...]
    f_c = _log_sigmoid(fg + bf_ref[...])
    t_i = lax.broadcasted_iota(jnp.int32, (L, L), 0)
    s_i = lax.broadcasted_iota(jnp.int32, (L, L), 1)
    causal = s_i <= t_i
    tri = causal.astype(f32)
    b_c = _dot(tri, f_c, HI)
    b_r = b_c.T[:SUBLANES, :]
    i_r = i_c.T[:SUBLANES, :]
    a_c = b_c + m_row[...]

    inter = _dot(qb, c_st[...].astype(bf16))
    qn = _dot(qb, n_st[...].astype(bf16))
    lane_head = lax.broadcasted_iota(jnp.int32, (1, W), 1) // M_DH
    intra = jnp.zeros((L, W), f32)
    w_full = jnp.zeros((L, W), f32)
    r_full = jnp.zeros((L, W), f32)
    for h in range(M_HEADS):
        hm = lane_head == h
        dmat = jnp.where(causal, b_c[:, h:h + 1] - b_r[h:h + 1, :] + i_r[h:h + 1, :], -jnp.inf)
        a_h = a_c[:, h:h + 1]
        mt = jnp.maximum(a_h, jnp.max(dmat, -1, keepdims=True))
        pm = jnp.exp(dmat - mt)
        s = _nt(jnp.where(hm, qb, jnp.zeros_like(qb)), kb) * pm
        w_h = jnp.exp(a_h - mt)
        den = w_h * qn[:, h:h + 1] + jnp.sum(s, -1, keepdims=True)
        r_h = 1.0 / jnp.maximum(jnp.abs(den), jnp.exp(-mt))
        intra = intra + _dot(s.astype(bf16), jnp.where(hm, vb, jnp.zeros_like(vb)))
        w_full = jnp.where(hm, w_h, w_full)
        r_full = jnp.where(hm, r_h, r_full)
    hout = (w_full * inter + intra) * r_full
    ym_ref[0] = (hout * jax.nn.sigmoid(o_raw)).astype(ym_ref.dtype)

    b_end = b_r[:, L - 1:L]
    g_r = b_end - b_r + i_r
    mc = m_col[...][:, :1]
    m_new = jnp.maximum(b_end + mc, jnp.max(g_r, -1, keepdims=True))
    w_old = jnp.exp(b_end + mc - m_new)
    w_tok = jnp.exp(g_r - m_new)
    w_tok_full = jnp.concatenate([jnp.broadcast_to(w_tok[h:h + 1, :], (M_DH, L)) for h in range(M_HEADS)], axis=0)
    w_old_full = jnp.concatenate([jnp.broadcast_to(w_old[h:h + 1, :], (M_DH, 1)) for h in range(M_HEADS)], axis=0)
    ktw = (kt * w_tok_full).astype(bf16)
    row_head = lax.broadcasted_iota(jnp.int32, (W, W), 0) // M_DH
    col_head = lax.broadcasted_iota(jnp.int32, (W, W), 1) // M_DH
    c_st[...] = w_old_full * c_st[...] + jnp.where(row_head == col_head, _dot(ktw, vb), 0.0)
    n_rows = lax.broadcasted_iota(jnp.int32, (W, LANES), 0) // M_DH
    n_cols = lax.broadcasted_iota(jnp.int32, (W, LANES), 1)
    ksum = _dot(ktw, jnp.ones((L, LANES), bf16))
    n_st[...] = w_old_full * n_st[...] + jnp.where(n_rows == n_cols, ksum, 0.0)
    eye = (lax.broadcasted_iota(jnp.int32, (SUBLANES, LANES), 0)
           == lax.broadcasted_iota(jnp.int32, (SUBLANES, LANES), 1)).astype(f32)
    m_row[...] = jnp.sum(eye * m_new, axis=0, keepdims=True)
    m_col[...] = jnp.broadcast_to(m_new, (SUBLANES, LANES))

    @pl.when(c == pl.num_programs(1) - 1)
    def _():
        c_out_ref[0] = c_st[...]
        n_out_ref[0] = n_st[...]
        m_out_ref[0] = m_col[...]


def mlstm_prompt(zm, cw, cb, wq, wk, wkt, wv, bi, bfg, L):
    B, S, _ = zm.shape
    W = M_WIDTH
    full = lambda a: pl.BlockSpec(a.shape, lambda b, c: (0,) * a.ndim)
    return pl.pallas_call(
        _mlstm_kernel,
        grid=(B, S // L),
        in_specs=[pl.BlockSpec((1, L, ZM_W), lambda b, c: (b, c, 0)),
                  full(cw), full(cb), full(wq), full(wk), full(wkt), full(wv), full(bi), full(bfg)],
        out_specs=[pl.BlockSpec((1, L, W), lambda b, c: (b, c, 0)),
                   pl.BlockSpec((1, W, W), lambda b, c: (b, 0, 0)),
                   pl.BlockSpec((1, W, LANES), lambda b, c: (b, 0, 0)),
                   pl.BlockSpec((1, SUBLANES, LANES), lambda b, c: (b, 0, 0))],
        out_shape=[jax.ShapeDtypeStruct((B, S, W), bf16),
                   jax.ShapeDtypeStruct((B, W, W), f32),
                   jax.ShapeDtypeStruct((B, W, LANES), f32),
                   jax.ShapeDtypeStruct((B, SUBLANES, LANES), f32)],
        scratch_shapes=[pltpu.VMEM((W, W), f32), pltpu.VMEM((W, LANES), f32),
                        pltpu.VMEM((1, LANES), f32), pltpu.VMEM((SUBLANES, LANES), f32),
                        pltpu.VMEM((SUBLANES, W), f32)],
        compiler_params=_params(("arbitrary", "arbitrary")),
        name="mlstm_prompt",
    )(zm, cw, cb, wq, wk, wkt, wv, bi, bfg)


def _moba_prompt_kernel(tab_ref, q_ref, k_ref, v_ref, o_ref, k16, v16, kmean, b_own, b_prev, sel_ref):
    S = k_ref.shape[1]
    nb = S // C_BLOCK
    T = C_BLOCK
    i = pl.program_id(1)
    key_i = lax.broadcasted_iota(jnp.int32, (T, T), 0)
    qry_i = lax.broadcasted_iota(jnp.int32, (T, T), 1)

    @pl.when(i == 0)
    def _():
        kf = k_ref[0]
        k16[...] = kf.astype(bf16)
        v16[...] = v_ref[0].astype(bf16)
        kmean[...] = jnp.sum(kf.reshape(nb, T, C_WIDTH), axis=1) * (1.0 / T)
        for h in range(C_HEADS):
            b_own[h] = _rel_bias(qry_i - key_i, tab_ref, h)
            b_prev[h] = _rel_bias(qry_i - key_i + T, tab_ref, h)

    qf = q_ref[0]
    lane_head = lax.broadcasted_iota(jnp.int32, (1, C_WIDTH), 1) // C_DH
    row_head = lax.broadcasted_iota(jnp.int32, (C_WIDTH, T), 0) // C_DH
    jio = lax.broadcasted_iota(jnp.int32, (nb, T), 0)
    valid = jio < i
    scale = C_DH ** -0.5
    start_own = pl.multiple_of(i * T, T)
    res = jnp.zeros((C_WIDTH, T), f32)
    for h in range(C_HEADS):
        hm = lane_head == h
        qm = jnp.where(hm, qf, 0.0)
        qm16 = (qm * scale).astype(bf16)
        gs = _nt(jnp.where(hm, kmean[...], 0.0), qf, HI)
        sel = jnp.zeros((nb, T), f32)
        for j in range(nb):
            vj = gs[j:j + 1, :]
            beats = jnp.where(gs > vj, 1.0, jnp.where((gs == vj) & (jio < j), 1.0, 0.0))
            cnt = jnp.sum(jnp.where(valid, beats, 0.0), axis=0, keepdims=True)
            sel = jnp.where(jio == j, jnp.where(cnt < C_TOPK, 1.0, 0.0), sel)
        sel_ref[...] = sel
        far_bias = tab_ref[h * REL_BUCKETS + REL_BUCKETS - 1]

        s = _nt(k16[pl.ds(start_own, T), :], qm16) + b_own[h]
        s = jnp.where(key_i <= qry_i, s, NEG)
        m0 = jnp.max(s, axis=0, keepdims=True)
        p = jnp.exp(s - m0)
        l0 = jnp.sum(p, axis=0, keepdims=True)
        acc0 = _tn(v16[pl.ds(start_own, T), :], p.astype(bf16))

        def body(j, carry):
            m, l, acc = carry
            start = pl.multiple_of(j * T, T)
            bias = jnp.where(j == i - 1, b_prev[h], far_bias)
            sj = _nt(k16[pl.ds(start, T), :], qm16) + bias
            sj = jnp.where(sel_ref[pl.ds(j, 1), :] > 0.5, sj, NEG)
            m_new = jnp.maximum(m, jnp.max(sj, axis=0, keepdims=True))
            alpha = jnp.exp(m - m_new)
            pj = jnp.exp(sj - m_new)
            l = alpha * l + jnp.sum(pj, axis=0, keepdims=True)
            acc = alpha * acc + _tn(v16[pl.ds(start, T), :], pj.astype(bf16))
            return m_new, l, acc

        m, l, acc = lax.fori_loop(0, i, body, (m0, l0, acc0))
        res = jnp.where(row_head == h, acc / l, res)
    o_ref[0] = res.T.astype(o_ref.dtype)


def moba_prompt(q, k, v, tab):
    B, S, W = q.shape
    nb = S // C_BLOCK
    return pl.pallas_call(
        _moba_prompt_kernel,
        grid=(B, nb),
        in_specs=[pl.BlockSpec(memory_space=pltpu.SMEM),
                  pl.BlockSpec((1, C_BLOCK, W), lambda b, i: (b, i, 0)),
                  pl.BlockSpec((1, S, W), lambda b, i: (b, 0, 0)),
                  pl.BlockSpec((1, S, W), lambda b, i: (b, 0, 0))],
        out_specs=pl.BlockSpec((1, C_BLOCK, W), lambda b, i: (b, i, 0)),
        out_shape=jax.ShapeDtypeStruct((B, S, W), bf16),
        scratch_shapes=[pltpu.VMEM((S, W), bf16), pltpu.VMEM((S, W), bf16), pltpu.VMEM((nb, W), f32),
                        pltpu.VMEM((C_HEADS, C_BLOCK, C_BLOCK), f32), pltpu.VMEM((C_HEADS, C_BLOCK, C_BLOCK), f32),
                        pltpu.VMEM((nb, C_BLOCK), f32)],
        compiler_params=_params(("arbitrary", "arbitrary")),
        name="moba_prompt",
    )(tab, q, k, v)


def _layer_out_kernel(x_ref, ya_ref, ym_ref, yc_ref, sg_ref, g1_ref, sh2_ref, sc2_ref, g2_ref,
                      wa_ref, wm_ref, wc_ref, wo_ref, l1g_ref, l1b_ref, l2g_ref, l2b_ref,
                      wg_ref, wu_ref, wd_ref, o_ref, x1_ref, h2_ref, f_ref, *, alpha):
    kf = pl.program_id(2)
    d = D_MODEL

    @pl.when(kf == 0)
    def _():
        pa = _dot(ya_ref[0], wa_ref[...])
        pm = _dot(ym_ref[0], wm_ref[...])
        pc = _dot(yc_ref[0], wc_ref[...])
        pre = (sg_ref[0, :, :d].astype(f32) * pa + sg_ref[0, :, d:2 * d].astype(f32) * pm
               + sg_ref[0, :, 2 * d:].astype(f32) * pc)
        mix = _dot(pre.astype(wo_ref.dtype), wo_ref[...])
        x1 = _layer_norm(alpha * x_ref[0] + g1_ref[0] * mix, l1g_ref[...], l1b_ref[...])
        x1_ref[...] = x1
        h2_ref[...] = (x1 * (1.0 + sc2_ref[0]) + sh2_ref[0]).astype(h2_ref.dtype)
        f_ref[...] = jnp.zeros_like(f_ref)

    h2 = h2_ref[...]
    t = _silu(_dot(h2, wg_ref[...])) * _dot(h2, wu_ref[...])
    f_ref[...] += _dot(t.astype(wd_ref.dtype), wd_ref[...])

    @pl.when(kf == pl.num_programs(2) - 1)
    def _():
        o_ref[0] = _layer_norm(alpha * x1_ref[...] + g2_ref[0] * f_ref[...], l2g_ref[...], l2b_ref[...])


def layer_out(x, ya, ym, yc, sg, mods, wa, wm, wc, wo, l1g, l1b, l2g, l2b, wg, wu, wd, tm, tf, alpha):
    B, S, d = x.shape
    dff = wg.shape[1]
    tok = lambda a: pl.BlockSpec((1, tm, a.shape[2]), lambda b, i, k: (b, i, 0))
    full = lambda a: pl.BlockSpec(a.shape, lambda b, i, k: (0,) * a.ndim)
    return pl.pallas_call(
        functools.partial(_layer_out_kernel, alpha=alpha),
        grid=(B, S // tm, dff // tf),
        in_specs=[tok(x), tok(ya), tok(ym), tok(yc), tok(sg),
                  _mod_spec(mods, tm, 2), _mod_spec(mods, tm, 3), _mod_spec(mods, tm, 4), _mod_spec(mods, tm, 5),
                  full(wa), full(wm), full(wc), full(wo), full(l1g), full(l1b), full(l2g), full(l2b),
                  pl.BlockSpec((d, tf), lambda b, i, k: (0, k)),
                  pl.BlockSpec((d, tf), lambda b, i, k: (0, k)),
                  pl.BlockSpec((tf, d), lambda b, i, k: (k, 0))],
        out_specs=pl.BlockSpec((1, tm, d), lambda b, i, k: (b, i, 0)),
        out_shape=jax.ShapeDtypeStruct((B, S, d), f32),
        scratch_shapes=[pltpu.VMEM((tm, d), f32), pltpu.VMEM((tm, d), bf16), pltpu.VMEM((tm, d), f32)],
        compiler_params=_params(("arbitrary", "arbitrary", "arbitrary")),
        name="layer_out",
    )(x, ya, ym, yc, sg, mods, mods, mods, mods, wa, wm, wc, wo, l1g, l1b, l2g, l2b, wg, wu, wd)


def _absorb_kernel(qh_ref, wabs_ref, o_ref):
    for h in range(A_HEADS):
        blk = slice(h * HEAD_BLK, (h + 1) * HEAD_BLK)
        o_ref[:, blk] = _dot(qh_ref[:, blk], wabs_ref[h], HI)


def mla_absorb(qh, wabs):
    return pl.pallas_call(_absorb_kernel, out_shape=jax.ShapeDtypeStruct(qh.shape, f32),
                          compiler_params=_params(None), name="mla_absorb")(qh, wabs)


def _mla_decode_kernel(pt_ref, qa_ref, qb_ref, latn_ref, krn_ref, *rest, pages):
    lat_refs = rest[:pages]
    kr_refs = rest[pages:2 * pages]
    o_ref = rest[2 * pages]
    m_s, l_s, acc_s = rest[2 * pages + 1:]
    c = pl.program_id(1)
    qa = qa_ref[0]
    qr = qb_ref[0][:, A_NOPE:A_NOPE + A_ROPE]

    @pl.when(c == 0)
    def _():
        latn = latn_ref[0]
        s0 = jnp.sum(qa * latn, -1, keepdims=True) + jnp.sum(qr * krn_ref[0], -1, keepdims=True)
        m_s[...] = s0
        l_s[...] = jnp.ones_like(l_s)
        acc_s[...] = jnp.broadcast_to(latn, acc_s.shape)

    lat = jnp.concatenate([r[...] for r in lat_refs], axis=0).astype(bf16)
    kr = jnp.concatenate([r[...] for r in kr_refs], axis=0).astype(bf16)
    s = _nt(qa.astype(bf16), lat) + _nt(qr.astype(bf16), kr)
    m_old = m_s[...]
    m_new = jnp.maximum(m_old, jnp.max(s, -1, keepdims=True))
    alpha = jnp.exp(m_old - m_new)
    p = jnp.exp(s - m_new)
    l_s[...] = alpha * l_s[...] + jnp.sum(p, -1, keepdims=True)
    acc_s[...] = alpha * acc_s[...] + _dot(p.astype(bf16), lat)
    m_s[...] = m_new

    @pl.when(c == pl.num_programs(1) - 1)
    def _():
        o_ref[0] = acc_s[...] / l_s[...]


def mla_decode(page_table, qa, qb, latn, krn, cache_lat, cache_kr, layer, pages):
    DB, n_pages = page_table.shape
    steps = n_pages // pages
    page_spec = lambda w, j: pl.BlockSpec(
        (None, None, PAGE_SIZE, w), lambda b, c, pt: (layer, pt[b, c * pages + j], 0, 0))
    seq = lambda a: pl.BlockSpec((1,) + a.shape[1:], lambda b, c, pt: (b, 0, 0))
    grid_spec = pltpu.PrefetchScalarGridSpec(
        num_scalar_prefetch=1,
        grid=(DB, steps),
        in_specs=[seq(qa), seq(qb), seq(latn), seq(krn)]
        + [page_spec(A_KV_LORA, j) for j in range(pages)]
        + [page_spec(A_ROPE, j) for j in range(pages)],
        out_specs=pl.BlockSpec((1, A_HEADS, A_KV_LORA), lambda b, c, pt: (b, 0, 0)),
        scratch_shapes=[pltpu.VMEM((A_HEADS, 1), f32), pltpu.VMEM((A_HEADS, 1), f32),
                        pltpu.VMEM((A_HEADS, A_KV_LORA), f32)],
    )
    return pl.pallas_call(
        functools.partial(_mla_decode_kernel, pages=pages),
        grid_spec=grid_spec,
        out_shape=jax.ShapeDtypeStruct((DB, A_HEADS, A_KV_LORA), f32),
        compiler_params=_params(("arbitrary", "arbitrary")),
        name="mla_decode",
    )(page_table, qa, qb, latn, krn, *([cache_lat] * pages), *([cache_kr] * pages))


def _value_up_kernel(o_ref, w_ref, y_ref):
    y_ref[...] = _dot(o_ref[...], w_ref[...], HI).astype(y_ref.dtype)


def mla_value_up(olat, wuv_bd):
    return pl.pallas_call(_value_up_kernel,
                          out_shape=jax.ShapeDtypeStruct((olat.shape[0], wuv_bd.shape[1]), bf16),
                          compiler_params=_params(None), name="mla_value_up")(olat, wuv_bd)


def _mlstm_s1_kernel(zm_ref, cs_ref, cw_ref, cb_ref, wq_ref, wk_ref, wv_ref, bi_ref, bf_ref,
                     q_ref, k_ref, v_ref, i_ref, f_ref):
    W = M_WIDTH
    zm = zm_ref[...]
    u = zm[:, :W]
    acc = u * cw_ref[M_CONV - 1:M_CONV, :]
    for j in range(M_CONV - 1):
        acc = acc + cs_ref[:, j * W:(j + 1) * W] * cw_ref[j:j + 1, :]
    uc = _silu(acc + cb_ref[...])
    q_ref[...] = _dot(uc, wq_ref[...], HI)
    k_ref[...] = _dot(uc, wk_ref[...], HI)
    v_ref[...] = _dot(u, wv_ref[...], HI)
    i_ref[...] = zm[:, 2 * W:2 * W + LANES] + bi_ref[...]
    f_ref[...] = _log_sigmoid(zm[:, 2 * W + LANES:] + bf_ref[...])


def mlstm_sample_proj(zm, conv_state, cw, cb, wq, wk, wv, bi, bfg):
    DB = zm.shape[0]
    shapes = [jax.ShapeDtypeStruct((DB, M_WIDTH), f32)] * 3 + [jax.ShapeDtypeStruct((DB, LANES), f32)] * 2
    return pl.pallas_call(_mlstm_s1_kernel, out_shape=shapes, compiler_params=_params(None),
                          name="mlstm_sample_proj")(zm, conv_state, cw, cb, wq, wk, wv, bi, bfg)


def _mlstm_s2_kernel(qc_ref, kc_ref, kr_ref, vr_ref, or_ref, c0_ref, nc_ref, nr_ref, i_ref, f_ref, m_ref,
                     h_ref, c1_ref, n1_ref, m1_ref):
    qc = qc_ref[...]
    kc = kc_ref[...]
    kr = kr_ref[...]
    vr = vr_ref[...]
    c0 = c0_ref[...]
    i_pre = i_ref[...]
    a = f_ref[...] + m_ref[...]
    mt = jnp.maximum(a, i_pre)
    w_old = jnp.exp(a - mt)
    w_tok = jnp.exp(i_pre - mt)
    s = jnp.sum(qc * kc, axis=1, keepdims=True) * w_tok
    qn = jnp.sum(qc * nc_ref[...], axis=1, keepdims=True)
    qC = jnp.sum(qc * c0, axis=1, keepdims=True)
    num = w_old * qC + s * vr
    den = w_old * qn + s
    h = num / jnp.maximum(jnp.abs(den), jnp.exp(-mt))
    h_ref[...] = h * jax.nn.sigmoid(or_ref[...])
    c1_ref[...] = w_old * c0 + w_tok * (kc * vr)
    n1_ref[...] = w_old * nr_ref[...] + w_tok * kr
    m1_ref[...] = mt


def mlstm_sample_cell(qc, kc, kr, vr, orow, c0, nc, nr, i_pre, log_f, m0, rows):
    R = qc.shape[0]
    spec = lambda a: pl.BlockSpec((rows,) + a.shape[1:], lambda r: (r, 0, 0))
    ins = (qc, kc, kr, vr, orow, c0, nc, nr, i_pre, log_f, m0)
    shapes = [jax.ShapeDtypeStruct((R, 1, M_DH), f32), jax.ShapeDtypeStruct((R, M_DH, M_DH), f32),
              jax.ShapeDtypeStruct((R, 1, M_DH), f32), jax.ShapeDtypeStruct((R, 1, 1), f32)]
    return pl.pallas_call(
        _mlstm_s2_kernel,
        grid=(R // rows,),
        in_specs=[spec(a) for a in ins],
        out_specs=[spec(s) for s in shapes],
        out_shape=shapes,
        compiler_params=_params(("arbitrary",)),
        name="mlstm_sample_cell",
    )(*ins)


def _moba_gate_kernel(pt_ref, q_ref, *rest, pages):
    k_refs = rest[:pages]
    idx_ref = rest[pages]
    bsum = rest[pages + 1]
    c = pl.program_id(1)
    ppb = C_BLOCK // PAGE_SIZE
    sums = [jnp.sum(r[...], axis=0, keepdims=True) for r in k_refs]
    blocks = [functools.reduce(lambda a, b: a + b, sums[j * ppb:(j + 1) * ppb]) for j in range(pages // ppb)]
    nblk = len(blocks)
    bsum[pl.ds(pl.multiple_of(c * nblk, nblk), nblk), :] = jnp.concatenate(blocks, axis=0)

    @pl.when(c == pl.num_programs(1) - 1)
    def _():
        nb = bsum.shape[0]
        prod = bsum[...] * (1.0 / C_BLOCK) * q_ref[0]
        ind = (lax.broadcasted_iota(jnp.int32, (C_WIDTH, LANES), 0) // C_DH
               == lax.broadcasted_iota(jnp.int32, (C_WIDTH, LANES), 1)).astype(f32)
        g = _dot(prod, ind, HI).T[:SUBLANES, :]
        lane = lax.broadcasted_iota(jnp.int32, g.shape, 1).astype(f32)
        out_lane = lax.broadcasted_iota(jnp.int32, (SUBLANES, LANES), 1)
        out = jnp.zeros((SUBLANES, LANES), f32)
        for r in range(C_TOPK):
            mx = jnp.max(g, -1, keepdims=True)
            ix = jnp.min(jnp.where(g == mx, lane, float(nb)), -1, keepdims=True)
            out = jnp.where(out_lane == r, ix, out)
            g = jnp.where(lane == ix, -jnp.inf, g)
        idx_ref[0] = out.astype(jnp.int32)


def moba_gate(page_table, q, cache_k, layer, pages):
    DB, n_pages = page_table.shape
    steps = n_pages // pages
    nbp = n_pages * PAGE_SIZE // C_BLOCK
    grid_spec = pltpu.PrefetchScalarGridSpec(
        num_scalar_prefetch=1,
        grid=(DB, steps),
        in_specs=[pl.BlockSpec((1, 1, C_WIDTH), lambda b, c, pt: (b, 0, 0))]
        + [pl.BlockSpec((None, None, PAGE_SIZE, C_WIDTH),
                        functools.partial(lambda b, c, pt, j: (layer, pt[b, c * pages + j], 0, 0), j=j))
           for j in range(pages)],
        out_specs=pl.BlockSpec((1, SUBLANES, LANES), lambda b, c, pt: (b, 0, 0)),
        scratch_shapes=[pltpu.VMEM((nbp, C_WIDTH), f32)],
    )
    return pl.pallas_call(
        functools.partial(_moba_gate_kernel, pages=pages),
        grid_spec=grid_spec,
        out_shape=jax.ShapeDtypeStruct((DB, SUBLANES, LANES), jnp.int32),
        compiler_params=_params(("arbitrary", "arbitrary")),
        name="moba_gate",
    )(page_table, q, *([cache_k] * pages))


def _moba_decode_kernel(pt_ref, idx_ref, tab_ref, q_ref, kn_ref, vn_ref, *rest, n_sel, past):
    k_refs = rest[:n_sel]
    v_refs = rest[n_sel:2 * n_sel]
    o_ref = rest[2 * n_sel]
    b = pl.program_id(0)
    h = pl.program_id(1)
    ppb = C_BLOCK // PAGE_SIZE
    scale = C_DH ** -0.5
    lane = lax.broadcasted_iota(jnp.int32, (1, C_WIDTH), 1)
    hm = lane // C_DH == h
    q = jnp.where(hm, q_ref[0], 0.0)
    kcat = jnp.concatenate([r[...] for r in k_refs], axis=0).astype(bf16)
    vcat = jnp.concatenate([r[...] for r in v_refs], axis=0).astype(bf16)
    q8 = jnp.broadcast_to((q * scale).astype(bf16), (SUBLANES, C_WIDTH))
    s = _nt(q8, kcat)[:1, :]
    t = lax.broadcasted_iota(jnp.int32, (1, C_BLOCK), 1)
    bias = [_rel_bias(past - (idx_ref[(b * C_HEADS + h) * C_TOPK + j] * C_BLOCK + t), tab_ref, h)
            for j in range(n_sel // ppb)]
    s = s + jnp.concatenate(bias, axis=1)
    s_self = jnp.sum(q * kn_ref[0], -1, keepdims=True) * scale + tab_ref[h * REL_BUCKETS]
    m = jnp.maximum(jnp.max(s, -1, keepdims=True), s_self)
    p = jnp.exp(s - m)
    p_self = jnp.exp(s_self - m)
    l = jnp.sum(p, -1, keepdims=True) + p_self
    pv = _dot(jnp.broadcast_to(p.astype(bf16), (SUBLANES, p.shape[1])), vcat)[:1, :]
    o = jnp.where(hm, (pv + p_self * vn_ref[0]) / l, 0.0)

    @pl.when(h == 0)
    def _():
        o_ref[0] = o

    @pl.when(h > 0)
    def _():
        o_ref[0] += o


def moba_decode(page_table, idx_flat, tab, q, kn, vn, cache_k, cache_v, layer):
    DB, n_pages = page_table.shape
    ppb = C_BLOCK // PAGE_SIZE
    n_sel = C_TOPK * ppb
    past = n_pages * PAGE_SIZE

    def page_spec(j):
        def imap(b, h, pt, idx):
            blk = idx[(b * C_HEADS + h) * C_TOPK + j // ppb]
            return (layer, pt[b, blk * ppb + j % ppb], 0, 0)
        return pl.BlockSpec((None, None, PAGE_SIZE, C_WIDTH), imap)

    seq = pl.BlockSpec((1, 1, C_WIDTH), lambda b, h, pt, idx: (b, 0, 0))
    grid_spec = pltpu.PrefetchScalarGridSpec(
        num_scalar_prefetch=2,
        grid=(DB, C_HEADS),
        in_specs=[pl.BlockSpec(memory_space=pltpu.SMEM), seq, seq, seq]
        + [page_spec(j) for j in range(n_sel)] + [page_spec(j) for j in range(n_sel)],
        out_specs=seq,
    )
    return pl.pallas_call(
        functools.partial(_moba_decode_kernel, n_sel=n_sel, past=past),
        grid_spec=grid_spec,
        out_shape=jax.ShapeDtypeStruct((DB, 1, C_WIDTH), f32),
        compiler_params=_params(("arbitrary", "arbitrary")),
        name="moba_decode",
    )(page_table, idx_flat, tab, q, kn, vn, *([cache_k] * n_sel), *([cache_v] * n_sel))


def _pack_w_in(w):
    d = w.shape[0]
    z = lambda n: jnp.zeros((d, n), w.dtype)
    o = 0
    parts = {}
    for name, size in (("cq", A_Q_LORA), ("ckv", A_KV_LORA), ("kr", A_ROPE), ("u", M_WIDTH), ("o", M_WIDTH),
                       ("i", M_HEADS), ("f", M_HEADS), ("qkv", 3 * C_WIDTH), ("g", 3 * D_MODEL)):
        parts[name] = w[:, o:o + size]
        o += size
    kr = parts["kr"]
    half = A_ROPE // 2
    pad = LANES - A_NOPE - A_ROPE
    krp = jnp.concatenate([z(A_NOPE), kr, z(pad)], axis=1)
    krs = jnp.concatenate([z(A_NOPE), kr[:, half:], kr[:, :half], z(pad)], axis=1)
    return jnp.concatenate([parts["cq"], parts["ckv"], krp, krs, parts["u"], parts["o"],
                            parts["i"], z(LANES - M_HEADS), parts["f"], z(LANES - M_HEADS),
                            parts["qkv"], parts["g"]], axis=1)


def _pack_w_q(w):
    r = w.shape[0]
    w3 = w.reshape(r, A_HEADS, A_NOPE + A_ROPE)
    half = A_ROPE // 2
    pad = jnp.zeros((r, A_HEADS, LANES - A_NOPE - A_ROPE), w.dtype)
    nope, rope = w3[..., :A_NOPE], w3[..., A_NOPE:]
    w1 = jnp.concatenate([nope, rope, pad], axis=-1)
    w2 = jnp.concatenate([jnp.zeros_like(nope), rope[..., half:], rope[..., :half], pad], axis=-1)
    return w1.reshape(r, -1), w2.reshape(r, -1)


def _rope_tables(pos):
    inv = ROPE_THETA ** (-jnp.arange(0, A_ROPE, 2, dtype=f32) / A_ROPE)
    ang = pos.astype(f32)[:, None] * inv[None, :]
    cos, sin = jnp.cos(ang), jnp.sin(ang)
    n = pos.shape[0]
    pad = jnp.zeros((n, LANES - A_NOPE - A_ROPE), f32)
    ct = jnp.concatenate([jnp.ones((n, A_NOPE), f32), cos, cos, pad], axis=1)
    st = jnp.concatenate([jnp.zeros((n, A_NOPE), f32), -sin, sin, pad], axis=1)
    return ct, st


def _block_diag(w):
    hN, a, b = w.shape
    eye = jnp.eye(hN, dtype=w.dtype)
    return (eye[:, None, :, None] * w[:, :, None, :]).reshape(hN * a, hN * b)


def _lane_pad(v, width=LANES):
    return jnp.pad(v, ((0, 0), (0, width - v.shape[-1])))


def kernel(x_prompt, x_sample, cache_mla_latent, cache_mla_krope, cache_moba_k, cache_moba_v, state_mlstm_C, state_mlstm_n, state_mlstm_m, state_mlstm_conv, page_table, c_prompt, c_sample, rel_table, w_ada, b_ada, w_in, g_q_norm, w_q_up, g_kv_norm, w_kv_up, conv_w, conv_b, w_mq, w_mk, w_mv, b_i, b_f, w_br_a, w_br_m, w_br_c, w_out, ln1_g, ln1_b, w_ff_gate, w_ff_up, w_ff_down, ln2_g, ln2_b):
    B, S, d = x_prompt.shape
    DB, T, _ = x_sample.shape
    depth = w_in.shape[0]
    n_pool = cache_mla_latent.shape[1]
    n_pages = page_table.shape[1]
    past = n_pages * PAGE_SIZE
    assert T == 1 and past % C_BLOCK == 0 and past // C_BLOCK >= C_TOPK and S % C_BLOCK == 0
    alpha = (2 * depth) ** 0.25
    dff = w_ff_gate.shape[2]

    mods_all = ada_mods(jnp.concatenate([c_prompt, c_sample], axis=0), w_ada, b_ada)
    ct_p, st_p = _rope_tables(jnp.arange(S, dtype=jnp.int32))
    ct_s, st_s = _rope_tables(jnp.full((DB,), past, jnp.int32))
    tab = rel_table.T.reshape(-1)
    cache_k4 = cache_moba_k.reshape(depth, n_pool, PAGE_SIZE, C_WIDTH)
    cache_v4 = cache_moba_v.reshape(depth, n_pool, PAGE_SIZE, C_WIDTH)
    xs = x_sample.reshape(1, DB, d)
    xp = x_prompt

    new_p = [[] for _ in range(8)]
    new_s = [[] for _ in range(8)]
    for l in range(depth):
        w_pack = _pack_w_in(w_in[l]).astype(bf16)
        wq1, wq2 = _pack_w_q(w_q_up[l])
        w_uk = w_kv_up[l][..., :A_NOPE]
        w_uv = w_kv_up[l][..., A_NOPE:]
        wuk = jnp.concatenate([w_uk, jnp.zeros_like(w_uk)], axis=-1).reshape(A_KV_LORA, -1)
        wuv = w_uv.reshape(A_KV_LORA, -1)
        gq = g_q_norm[l][None, :]
        gkv = g_kv_norm[l][None, :]
        wq_bd = _block_diag(w_mq[l])
        wk_bd = _block_diag(w_mk[l]) * (M_DH ** -0.5)
        wv_bd = _block_diag(w_mv[l])
        bi = _lane_pad(b_i[l][None, :])
        bfg = _lane_pad(b_f[l][None, :])
        cw = conv_w[l]
        cb = conv_b[l][None, :]
        tail_w = (w_br_a[l].astype(bf16), w_br_m[l].astype(bf16), w_br_c[l].astype(bf16), w_out[l].astype(bf16),
                  ln1_g[l][None, :], ln1_b[l][None, :], ln2_g[l][None, :], ln2_b[l][None, :],
                  w_ff_gate[l].astype(bf16), w_ff_up[l].astype(bf16), w_ff_down[l].astype(bf16))

        mods_p = mods_all[l, :B].reshape(B, 1, 6 * d)
        za, zm, qc, kc, vc, sg = in_proj(xp, mods_p, w_pack, tm=512)
        qh, kh, vh, lat, krope = mla_prep(za, ct_p, st_p, gq, gkv, wq1.astype(bf16), wq2.astype(bf16),
                                          wuk.astype(bf16), wuv.astype(bf16), tm=512, qk_dtype=bf16)
        ya = mla_attn(qh, kh, vh, tq=512)
        ym, c_st, n_st, m_st = mlstm_prompt(zm, cw, cb, wq_bd.astype(bf16), wk_bd.astype(bf16),
                                            wk_bd.T.astype(bf16), wv_bd.astype(bf16), bi, bfg, L=256)
        yc = moba_prompt(qc, kc, vc, tab)
        xp_new = layer_out(xp, ya, ym, yc, sg, mods_p, *tail_w, tm=256, tf=dff // 2, alpha=alpha)
        C1 = jnp.stack([c_st[:, h * M_DH:(h + 1) * M_DH, h * M_DH:(h + 1) * M_DH] for h in range(M_HEADS)], axis=1)
        n1 = jnp.stack([n_st[:, h * M_DH:(h + 1) * M_DH, h] for h in range(M_HEADS)], axis=1)
        m1 = m_st[:, :M_HEADS, 0]
        u_p = zm[:, :, :M_WIDTH]
        conv_p = jnp.pad(u_p, ((0, 0), (M_CONV - 1, 0), (0, 0)))[:, -(M_CONV - 1):]
        for lst, val in zip(new_p, (lat, krope, kc.reshape(B, S, C_HEADS, C_DH), vc.reshape(B, S, C_HEADS, C_DH),
                                    C1, n1, m1, conv_p)):
            lst.append(val)
        xp = xp_new

        mods_s = mods_all[l, B:].reshape(1, DB, 6 * d)
        za, zm, qc, kc, vc, sg = in_proj(xs, mods_s, w_pack, tm=DB)
        qh, _, _, lat, krope = mla_prep(za, ct_s, st_s, gq, gkv, wq1, wq2, wuk, wuv, tm=DB, qk_dtype=f32)
        wabs = jnp.concatenate([jnp.transpose(w_uk, (1, 2, 0)),
                                jnp.zeros((A_HEADS, HEAD_BLK - A_NOPE, A_KV_LORA), f32)], axis=1)
        qh2 = qh.reshape(DB, A_HEADS * HEAD_BLK)
        qabs = mla_absorb(qh2, wabs)
        olat = mla_decode(page_table, qabs.reshape(DB, A_HEADS, A_KV_LORA), qh2.reshape(DB, A_HEADS, HEAD_BLK),
                          lat.reshape(DB, 1, A_KV_LORA), krope.reshape(DB, 1, A_ROPE),
                          cache_mla_latent, cache_mla_krope, l, pages=16)
        wuv_bd = _block_diag(jnp.transpose(w_uv, (1, 0, 2)))
        ya = mla_value_up(olat.reshape(DB, A_HEADS * A_KV_LORA), wuv_bd)

        zm2 = zm.reshape(DB, ZM_W)
        conv_state = state_mlstm_conv[l]
        q_m, k_m, v_m, i_pre, log_f = mlstm_sample_proj(zm2, conv_state.reshape(DB, -1), cw, cb,
                                                        wq_bd, wk_bd, wv_bd, bi, bfg)
        R = DB * M_HEADS
        col = lambda a: a.reshape(R, M_DH, 1)
        rowv = lambda a: a.reshape(R, 1, M_DH)
        sc = lambda a: a.reshape(R, 1, 1)
        h_m, C1, n1, m1 = mlstm_sample_cell(
            col(q_m), col(k_m), rowv(k_m), rowv(v_m), rowv(zm2[:, M_WIDTH:2 * M_WIDTH]),
            state_mlstm_C[l].reshape(R, M_DH, M_DH), col(state_mlstm_n[l]), rowv(state_mlstm_n[l]),
            sc(i_pre[:, :M_HEADS]), sc(log_f[:, :M_HEADS]), sc(state_mlstm_m[l]), rows=32)
        ym = h_m.reshape(DB, M_WIDTH).astype(bf16)
        u_s = zm2[:, :M_WIDTH]
        conv_s = jnp.concatenate([conv_state[:, 1:], u_s[:, None, :]], axis=1)

        q3 = qc.reshape(DB, 1, C_WIDTH)
        idx = moba_gate(page_table, q3, cache_k4, l, pages=16)
        idx_flat = idx[:, :C_HEADS, :C_TOPK].reshape(-1)
        yc = moba_decode(page_table, idx_flat, tab, q3, kc.reshape(DB, 1, C_WIDTH), vc.reshape(DB, 1, C_WIDTH),
                         cache_k4, cache_v4, l)
        xs_new = layer_out(xs, ya.reshape(1, DB, -1), ym.reshape(1, DB, -1), yc.reshape(1, DB, -1).astype(bf16),
                           sg, mods_s, *tail_w, tm=DB, tf=dff // 2, alpha=alpha)
        for lst, val in zip(new_s, (lat.reshape(DB, 1, A_KV_LORA), krope.reshape(DB, 1, A_ROPE),
                                    kc.reshape(DB, 1, C_HEADS, C_DH), vc.reshape(DB, 1, C_HEADS, C_DH),
                                    C1.reshape(DB, M_HEADS, M_DH, M_DH), n1.reshape(DB, M_HEADS, M_DH),
                                    m1.reshape(DB, M_HEADS), conv_s)):
            lst.append(val)
        xs = xs_new

    outs_p = [jnp.stack(a) for a in new_p]
    outs_s = [jnp.stack(a) for a in new_s]
    return (xp, xs.reshape(DB, T, d), *outs_p, *outs_s)
```

```python
import functools
import math

import jax
import jax.numpy as jnp
from jax import lax
from jax.experimental import pallas as pl
from jax.experimental.pallas import tpu as pltpu

f32 = jnp.float32
bf16 = jnp.bfloat16
HI = lax.Precision.HIGHEST

D_MODEL = 1024
PAGE_SIZE = 128
A_HEADS = 8
A_NOPE = 64
A_ROPE = 32
A_VDIM = 64
A_Q_LORA = 256
A_KV_LORA = 128
ROPE_THETA = 10000.0
M_HEADS = 4
M_DH = 64
M_WIDTH = M_HEADS * M_DH
M_CONV = 4
C_HEADS = 4
C_DH = 64
C_WIDTH = C_HEADS * C_DH
C_BLOCK = 256
C_TOPK = 3
REL_BUCKETS = 32
REL_MAX_DIST = 128
LN_EPS = 1e-5
RMS_EPS = 1e-6
LANES = 128
SUBLANES = 8
VMEM_LIMIT = 56 * 1024 * 1024
NEG = -1e30

ZA_W = A_Q_LORA + A_KV_LORA + 2 * LANES
ZM_W = 2 * M_WIDTH + 2 * LANES
ZC_W = 3 * C_WIDTH
ZG_W = 3 * D_MODEL
PACK_W = ZA_W + ZM_W + ZC_W + ZG_W
HEAD_BLK = LANES


def _rel_thresholds():
    exact = REL_BUCKETS // 2
    out = []
    for k in range(1, REL_BUCKETS - exact):
        n = exact
        while int(math.log(n / exact) / math.log(REL_MAX_DIST / exact) * (REL_BUCKETS - exact)) < k:
            n += 1
        out.append(n)
    return tuple(out)


REL_THRESH = _rel_thresholds()


def _nt(a, b, precision=None):
    return lax.dot_general(a, b, (((1,), (1,)), ((), ())), precision=precision, preferred_element_type=f32)


def _tn(a, b):
    return lax.dot_general(a, b, (((0,), (0,)), ((), ())), preferred_element_type=f32)


def _dot(a, b, precision=None):
    return jnp.dot(a, b, precision=precision, preferred_element_type=f32)


def _silu(x):
    return x * jax.nn.sigmoid(x)


def _layer_norm(x, g, b):
    mu = jnp.mean(x, -1, keepdims=True)
    xc = x - mu
    var = jnp.mean(xc * xc, -1, keepdims=True)
    return xc * lax.rsqrt(var + LN_EPS) * g + b


def _rms_norm(x, g):
    return x * lax.rsqrt(jnp.mean(x * x, -1, keepdims=True) + RMS_EPS) * g


def _rel_bias(dist, tab_ref, h):
    n = jnp.maximum(dist, 0)
    exact = REL_BUCKETS // 2
    large = jnp.full(n.shape, exact, jnp.int32)
    for t in REL_THRESH:
        large = large + (n >= t).astype(jnp.int32)
    bucket = jnp.where(n < exact, n, large)
    val = jnp.full(n.shape, tab_ref[h * REL_BUCKETS + REL_BUCKETS - 1], f32)
    for b in range(REL_BUCKETS - 1):
        val = jnp.where(bucket == b, tab_ref[h * REL_BUCKETS + b], val)
    return val


def _params(sem):
    return pltpu.CompilerParams(dimension_semantics=sem, vmem_limit_bytes=VMEM_LIMIT)


def _ada_kernel(c_ref, w_ref, b_ref, o_ref):
    c = c_ref[...]
    o_ref[...] = _dot(_silu(c), w_ref[...], HI) + b_ref[...]


def ada_mods(c_all, w_ada, b_ada):
    depth, d, n = w_ada.shape
    rows = c_all.shape[0]
    tn = n // 4
    return pl.pallas_call(
        _ada_kernel,
        grid=(depth, n // tn),
        in_specs=[
            pl.BlockSpec((rows, d), lambda l, j: (0, 0)),
            pl.BlockSpec((None, d, tn), lambda l, j: (l, 0, j)),
            pl.BlockSpec((None, 1, tn), lambda l, j: (l, 0, j)),
        ],
        out_specs=pl.BlockSpec((None, rows, tn), lambda l, j: (l, 0, j)),
        out_shape=jax.ShapeDtypeStruct((depth, rows, n), f32),
        compiler_params=_params(("arbitrary", "arbitrary")),
        name="ada_mods",
    )(c_all, w_ada, b_ada.reshape(depth, 1, n))


def _mod_spec(mods, tm, col):
    per_row = mods.shape[1] != 1
    rows = tm if per_row else 1
    return pl.BlockSpec((1, rows, D_MODEL), lambda b, i, *_: (b, i if per_row else 0, col))


def _in_proj_kernel(x_ref, sh_ref, sc_ref, w_ref, za_ref, zm_ref, q_ref, k_ref, v_ref, sg_ref):
    h = (x_ref[0] * (1.0 + sc_ref[0]) + sh_ref[0]).astype(w_ref.dtype)

    def mm(c0, c1):
        return _dot(h, w_ref[:, c0:c1])

    c = 0
    za_ref[0] = mm(c, c + ZA_W)
    c += ZA_W
    zm_ref[0] = mm(c, c + ZM_W)
    c += ZM_W
    q_ref[0] = mm(c, c + C_WIDTH)
    k_ref[0] = mm(c + C_WIDTH, c + 2 * C_WIDTH)
    v_ref[0] = mm(c + 2 * C_WIDTH, c + 3 * C_WIDTH)
    c += ZC_W
    sg_ref[0] = jax.nn.sigmoid(mm(c, c + ZG_W)).astype(sg_ref.dtype)


def in_proj(x, mods, w_pack, tm):
    B, S, d = x.shape
    widths = (ZA_W, ZM_W, C_WIDTH, C_WIDTH, C_WIDTH, ZG_W)
    dts = (f32, f32, f32, f32, f32, bf16)
    return pl.pallas_call(
        _in_proj_kernel,
        grid=(B, S // tm),
        in_specs=[
            pl.BlockSpec((1, tm, d), lambda b, i: (b, i, 0)),
            _mod_spec(mods, tm, 0),
            _mod_spec(mods, tm, 1),
            pl.BlockSpec((d, PACK_W), lambda b, i: (0, 0)),
        ],
        out_specs=[pl.BlockSpec((1, tm, w), lambda b, i: (b, i, 0)) for w in widths],
        out_shape=[jax.ShapeDtypeStruct((B, S, w), t) for w, t in zip(widths, dts)],
        compiler_params=_params(("arbitrary", "arbitrary")),
        name="in_proj",
    )(x, mods, mods, w_pack)


def _mla_prep_kernel(za_ref, ct_ref, st_ref, gq_ref, gkv_ref, wq1_ref, wq2_ref, wuk_ref, wuv_ref,
                     qh_ref, kh_ref, vh_ref, lat_ref, kr_ref, *, prec):
    za = za_ref[0]
    cq = za[:, :A_Q_LORA]
    ckv = za[:, A_Q_LORA:A_Q_LORA + A_KV_LORA]
    krp = za[:, A_Q_LORA + A_KV_LORA:A_Q_LORA + A_KV_LORA + LANES]
    krs = za[:, A_Q_LORA + A_KV_LORA + LANES:]
    ct = ct_ref[...]
    st = st_ref[...]
    scale = (A_NOPE + A_ROPE) ** -0.5
    cqn = _rms_norm(cq, gq_ref[...]).astype(wq1_ref.dtype)
    q1 = _dot(cqn, wq1_ref[...], prec)
    q2 = _dot(cqn, wq2_ref[...], prec)
    lat = _rms_norm(ckv, gkv_ref[...])
    lat_ref[0] = lat
    kro = krp * ct + krs * st
    kr_ref[0] = kro[:, A_NOPE:A_NOPE + A_ROPE]
    latc = lat.astype(wuk_ref.dtype)
    kn = _dot(latc, wuk_ref[...], prec)
    for h in range(A_HEADS):
        blk = slice(h * HEAD_BLK, (h + 1) * HEAD_BLK)
        qh_ref[0, :, blk] = ((q1[:, blk] * ct + q2[:, blk] * st) * scale).astype(qh_ref.dtype)
        kh_ref[0, :, blk] = (kn[:, blk] + kro).astype(kh_ref.dtype)
    vh_ref[0] = _dot(latc, wuv_ref[...], prec).astype(vh_ref.dtype)


def mla_prep(za, ctab, stab, gq, gkv, wq1, wq2, wuk, wuv, tm, qk_dtype):
    B, S, _ = za.shape
    full = lambda a: pl.BlockSpec(a.shape, lambda b, i: (0,) * a.ndim)
    tok = lambda w: pl.BlockSpec((1, tm, w), lambda b, i: (b, i, 0))
    widths = (A_HEADS * HEAD_BLK, A_HEADS * HEAD_BLK, A_HEADS * A_VDIM, A_KV_LORA, A_ROPE)
    dts = (qk_dtype, qk_dtype, qk_dtype, f32, f32)
    prec = HI if qk_dtype == f32 else None
    return pl.pallas_call(
        functools.partial(_mla_prep_kernel, prec=prec),
        grid=(B, S // tm),
        in_specs=[tok(ZA_W),
                  pl.BlockSpec((tm, LANES), lambda b, i: (i, 0)),
                  pl.BlockSpec((tm, LANES), lambda b, i: (i, 0)),
                  full(gq), full(gkv), full(wq1), full(wq2), full(wuk), full(wuv)],
        out_specs=[tok(w) for w in widths],
        out_shape=[jax.ShapeDtypeStruct((B, S, w), t) for w, t in zip(widths, dts)],
        compiler_params=_params(("arbitrary", "arbitrary")),
        name="mla_prep",
    )(za, ctab, stab, gq, gkv, wq1, wq2, wuk, wuv)


def _mla_attn_kernel(q_ref, k_ref, v_ref, o_ref, *, tq):
    S = q_ref.shape[1]
    lane = lax.broadcasted_iota(jnp.int32, (tq, 2 * A_VDIM), 1)
    for t in range(S // tq):
        kv_len = (t + 1) * tq
        row = t * tq + lax.broadcasted_iota(jnp.int32, (tq, kv_len), 0)
        col = lax.broadcasted_iota(jnp.int32, (tq, kv_len), 1)
        causal = col <= row
        vpair = v_ref[0, :kv_len, :]
        outs = []
        for hh in range(2):
            blk = slice(hh * HEAD_BLK, (hh + 1) * HEAD_BLK)
            s = _nt(q_ref[0, t * tq:(t + 1) * tq, blk], k_ref[0, :kv_len, blk])
            s = jnp.where(causal, s, -jnp.inf)
            m = jnp.max(s, -1, keepdims=True)
            p = jnp.exp(s - m)
            l = jnp.sum(p, -1, keepdims=True)
            outs.append(_dot(p.astype(vpair.dtype), vpair) / l)
        o_ref[0, t * tq:(t + 1) * tq, :] = jnp.where(lane < A_VDIM, outs[0], outs[1]).astype(o_ref.dtype)


def mla_attn(qh, kh, vh, tq):
    B, S, _ = qh.shape
    pairs = A_HEADS // 2
    return pl.pallas_call(
        functools.partial(_mla_attn_kernel, tq=tq),
        grid=(B, pairs),
        in_specs=[pl.BlockSpec((1, S, 2 * HEAD_BLK), lambda b, p: (b, 0, p)),
                  pl.BlockSpec((1, S, 2 * HEAD_BLK), lambda b, p: (b, 0, p)),
                  pl.BlockSpec((1, S, 2 * A_VDIM), lambda b, p: (b, 0, p))],
        out_specs=pl.BlockSpec((1, S, 2 * A_VDIM), lambda b, p: (b, 0, p)),
        out_shape=jax.ShapeDtypeStruct((B, S, A_HEADS * A_VDIM), bf16),
        compiler_params=_params(("arbitrary", "arbitrary")),
        name="mla_attn",
    )(qh, kh, vh)


def _log_sigmoid(x):
    return jnp.minimum(x, 0.0) - jnp.log1p(jnp.exp(-jnp.abs(x)))


def _mlstm_kernel(zm_ref, cw_ref, cb_ref, wq_ref, wk_ref, wkt_ref, wv_ref, bi_ref, bf_ref,
                  ym_ref, c_out_ref, n_out_ref, m_out_ref,
                  c_st, n_st, m_row, m_col, tail):
    L = zm_ref.shape[1]
    W = M_WIDTH
    c = pl.program_id(1)

    @pl.when(c == 0)
    def _():
        c_st[...] = jnp.zeros_like(c_st)
        n_st[...] = jnp.zeros_like(n_st)
        m_row[...] = jnp.zeros_like(m_row)
        m_col[...] = jnp.zeros_like(m_col)
        tail[...] = jnp.zeros_like(tail)

    zm = zm_ref[0]
    u = zm[:, :W]
    o_raw = zm[:, W:2 * W]
    ig = zm[:, 2 * W:2 * W + LANES]
    fg = zm[:, 2 * W + LANES:]

    rows = lax.broadcasted_iota(jnp.int32, (L, W), 0)
    tl = tail[...]
    acc = u * cw_ref[M_CONV - 1:M_CONV, :]
    for j in range(1, M_CONV):
        ru = pltpu.roll(u, j, 0)
        rt = jnp.concatenate([pltpu.roll(tl, j, 0), jnp.zeros((L - SUBLANES, W), f32)], axis=0)
        acc = acc + jnp.where(rows < j, rt, ru) * cw_ref[M_CONV - 1 - j:M_CONV - j, :]
    tail[...] = u[L - SUBLANES:, :]
    uc = _silu(acc + cb_ref[...]).astype(bf16)
    ub = u.astype(bf16)

    q = _dot(uc, wq_ref[...])
    k = _dot(uc, wk_ref[...])
    kt = _nt(wkt_ref[...], uc)
    v = _dot(ub, wv_ref[...])
    qb = q.astype(bf16)
    kb = k.astype(bf16)
    vb = v.astype(bf16)

    i_c = ig + bi_ref[...]
    f_c = _log_sigmoid(fg + bf_ref[...])
    t_i = lax.broadcasted_iota(jnp.int32, (L, L), 0)
    s_i = lax.broadcasted_iota(jnp.int32, (L, L), 1)
    causal = s_i <= t_i
    tri = causal.astype(f32)
    b_c = _dot(tri, f_c, HI)
    b_r = b_c.T[:SUBLANES, :]
    i_r = i_c.T[:SUBLANES, :]
    a_c = b_c + m_row[...]

    inter = _dot(qb, c_st[...].astype(bf16))
    qn = _dot(qb, n_st[...].astype(bf16))
    lane_head = lax.broadcasted_iota(jnp.int32, (1, W), 1) // M_DH
    intra = jnp.zeros((L, W), f32)
    w_full = jnp.zeros((L, W), f32)
    r_full = jnp.zeros((L, W), f32)
    for h in range(M_HEADS):
        hm = lane_head == h
        dmat = jnp.where(causal, b_c[:, h:h + 1] - b_r[h:h + 1, :] + i_r[h:h + 1, :], -jnp.inf)
        a_h = a_c[:, h:h + 1]
        mt = jnp.maximum(a_h, jnp.max(dmat, -1, keepdims=True))
        pm = jnp.exp(dmat - mt)
        s = _nt(jnp.where(hm, qb, jnp.zeros_like(qb)), kb) * pm
        w_h = jnp.exp(a_h - mt)
        den = w_h * qn[:, h:h + 1] + jnp.sum(s, -1, keepdims=True)
        r_h = 1.0 / jnp.maximum(jnp.abs(den), jnp.exp(-mt))
        intra = intra + _dot(s.astype(bf16), jnp.where(hm, vb, jnp.zeros_like(vb)))
        w_full = jnp.where(hm, w_h, w_full)
        r_full = jnp.where(hm, r_h, r_full)
    hout = (w_full * inter + intra) * r_full
    ym_ref[0] = (hout * jax.nn.sigmoid(o_raw)).astype(ym_ref.dtype)

    b_end = b_r[:, L - 1:L]
    g_r = b_end - b_r + i_r
    mc = m_col[...][:, :1]
    m_new = jnp.maximum(b_end + mc, jnp.max(g_r, -1, keepdims=True))
    w_old = jnp.exp(b_end + mc - m_new)
    w_tok = jnp.exp(g_r - m_new)
    w_tok_full = jnp.concatenate([jnp.broadcast_to(w_tok[h:h + 1, :], (M_DH, L)) for h in range(M_HEADS)], axis=0)
    w_old_full = jnp.concatenate([jnp.broadcast_to(w_old[h:h + 1, :], (M_DH, 1)) for h in range(M_HEADS)], axis=0)
    ktw = (kt * w_tok_full).astype(bf16)
    row_head = lax.broadcasted_iota(jnp.int32, (W, W), 0) // M_DH
    col_head = lax.broadcasted_iota(jnp.int32, (W, W), 1) // M_DH
    c_st[...] = w_old_full * c_st[...] + jnp.where(row_head == col_head, _dot(ktw, vb), 0.0)
    n_rows = lax.broadcasted_iota(jnp.int32, (W, LANES), 0) // M_DH
    n_cols = lax.broadcasted_iota(jnp.int32, (W, LANES), 1)
    ksum = _dot(ktw, jnp.ones((L, LANES), bf16))
    n_st[...] = w_old_full * n_st[...] + jnp.where(n_rows == n_cols, ksum, 0.0)
    eye = (lax.broadcasted_iota(jnp.int32, (SUBLANES, LANES), 0)
           == lax.broadcasted_iota(jnp.int32, (SUBLANES, LANES), 1)).astype(f32)
    m_row[...] = jnp.sum(eye * m_new, axis=0, keepdims=True)
    m_col[...] = jnp.broadcast_to(m_new, (SUBLANES, LANES))

    @pl.when(c == pl.num_programs(1) - 1)
    def _():
        c_out_ref[0] = c_st[...]
        n_out_ref[0] = n_st[...]
        m_out_ref[0] = m_col[...]


def mlstm_prompt(zm, cw, cb, wq, wk, wkt, wv, bi, bfg, L):
    B, S, _ = zm.shape
    W = M_WIDTH
    full = lambda a: pl.BlockSpec(a.shape, lambda b, c: (0,) * a.ndim)
    return pl.pallas_call(
        _mlstm_kernel,
        grid=(B, S // L),
        in_specs=[pl.BlockSpec((1, L, ZM_W), lambda b, c: (b, c, 0)),
                  full(cw), full(cb), full(wq), full(wk), full(wkt), full(wv), full(bi), full(bfg)],
        out_specs=[pl.BlockSpec((1, L, W), lambda b, c: (b, c, 0)),
                   pl.BlockSpec((1, W, W), lambda b, c: (b, 0, 0)),
                   pl.BlockSpec((1, W, LANES), lambda b, c: (b, 0, 0)),
                   pl.BlockSpec((1, SUBLANES, LANES), lambda b, c: (b, 0, 0))],
        out_shape=[jax.ShapeDtypeStruct((B, S, W), bf16),
                   jax.ShapeDtypeStruct((B, W, W), f32),
                   jax.ShapeDtypeStruct((B, W, LANES), f32),
                   jax.ShapeDtypeStruct((B, SUBLANES, LANES), f32)],
        scratch_shapes=[pltpu.VMEM((W, W), f32), pltpu.VMEM((W, LANES), f32),
                        pltpu.VMEM((1, LANES), f32), pltpu.VMEM((SUBLANES, LANES), f32),
                        pltpu.VMEM((SUBLANES, W), f32)],
        compiler_params=_params(("arbitrary", "arbitrary")),
        name="mlstm_prompt",
    )(zm, cw, cb, wq, wk, wkt, wv, bi, bfg)


def _moba_prompt_kernel(tab_ref, q_ref, k_ref, v_ref, o_ref,
                        k16, vt16, kmean, b_own, b_prev, sel_ref, qm_ref, acc_ref):
    S = k_ref.shape[1]
    nb = S // C_BLOCK
    T = C_BLOCK
    H = C_HEADS
    i = pl.program_id(1)
    key_i = lax.broadcasted_iota(jnp.int32, (T, T), 0)
    qry_i = lax.broadcasted_iota(jnp.int32, (T, T), 1)

    @pl.when(i == 0)
    def _():
        kf = k_ref[0]
        for j in range(nb):
            k16[j] = kf[j * T:(j + 1) * T, :].astype(bf16)
            vt16[j] = v_ref[0, j * T:(j + 1) * T, :].T.astype(bf16)
        kmean[...] = jnp.sum(kf.reshape(nb, T, C_WIDTH), axis=1) * (1.0 / T)
        for h in range(H):
            b_own[h] = _rel_bias(qry_i - key_i, tab_ref, h)
            b_prev[h] = _rel_bias(qry_i - key_i + T, tab_ref, h)

    qf = q_ref[0]
    lane_head = lax.broadcasted_iota(jnp.int32, (1, C_WIDTH), 1) // C_DH
    jio = lax.broadcasted_iota(jnp.int32, (nb, T), 0)
    valid = jio < i
    scale = C_DH ** -0.5
    for h in range(H):
        hm = lane_head == h
        qm_ref[h] = (jnp.where(hm, qf, 0.0) * scale).astype(bf16)
        gs = _nt(jnp.where(hm, kmean[...], 0.0), qf, HI)
        sel = jnp.zeros((nb, T), f32)
        for j in range(nb):
            vj = gs[j:j + 1, :]
            beats = jnp.where(gs > vj, 1.0, jnp.where((gs == vj) & (jio < j), 1.0, 0.0))
            cnt = jnp.sum(jnp.where(valid, beats, 0.0), axis=0, keepdims=True)
            sel = jnp.where(jio == j, jnp.where(cnt < C_TOPK, 1.0, 0.0), sel)
        sel_ref[h] = sel

    rows = lambda h: slice(h * C_DH, (h + 1) * C_DH)

    kblk = k16[i]
    vblk = vt16[i]
    ms, ls = [], []
    for h in range(H):
        s = _nt(kblk, qm_ref[h]) + b_own[h]
        s = jnp.where(key_i <= qry_i, s, NEG)
        m0 = jnp.max(s, axis=0, keepdims=True)
        p = jnp.exp(s - m0)
        ms.append(m0)
        ls.append(jnp.sum(p, axis=0, keepdims=True))
        acc_ref[rows(h), :] = _dot(vblk[rows(h), :], p.astype(bf16))

    def past_block(j, bias_of, carry):
        ms, ls = carry
        kblk = k16[j]
        vblk = vt16[j]
        ms_new, ls_new = [], []
        for h in range(H):
            sj = _nt(kblk, qm_ref[h]) + bias_of(h)
            sj = jnp.where(sel_ref[h, pl.ds(j, 1), :] > 0.5, sj, NEG)
            m_new = jnp.maximum(ms[h], jnp.max(sj, axis=0, keepdims=True))
            alpha = jnp.exp(ms[h] - m_new)
            pj = jnp.exp(sj - m_new)
            ms_new.append(m_new)
            ls_new.append(alpha * ls[h] + jnp.sum(pj, axis=0, keepdims=True))
            acc_ref[rows(h), :] = alpha * acc_ref[rows(h), :] + _dot(vblk[rows(h), :], pj.astype(bf16))
        return tuple(ms_new), tuple(ls_new)

    carry = (tuple(ms), tuple(ls))
    carry = lax.cond(i >= 1, lambda c: past_block(i - 1, lambda h: b_prev[h], c), lambda c: c, carry)
    ms, ls = lax.fori_loop(
        0, jnp.maximum(i - 1, 0),
        lambda j, c: past_block(j, lambda h: tab_ref[h * REL_BUCKETS + REL_BUCKETS - 1], c), carry)
    res = jnp.concatenate([acc_ref[rows(h), :] / ls[h] for h in range(H)], axis=0)
    o_ref[0] = res.T.astype(o_ref.dtype)


def moba_prompt(q, k, v, tab):
    B, S, W = q.shape
    nb = S // C_BLOCK
    T = C_BLOCK
    return pl.pallas_call(
        _moba_prompt_kernel,
        grid=(B, nb),
        in_specs=[pl.BlockSpec(memory_space=pltpu.SMEM),
                  pl.BlockSpec((1, T, W), lambda b, i: (b, i, 0)),
                  pl.BlockSpec((1, S, W), lambda b, i: (b, 0, 0)),
                  pl.BlockSpec((1, S, W), lambda b, i: (b, 0, 0))],
        out_specs=pl.BlockSpec((1, T, W), lambda b, i: (b, i, 0)),
        out_shape=jax.ShapeDtypeStruct((B, S, W), bf16),
        scratch_shapes=[pltpu.VMEM((nb, T, W), bf16), pltpu.VMEM((nb, W, T), bf16), pltpu.VMEM((nb, W), f32),
                        pltpu.VMEM((C_HEADS, T, T), f32), pltpu.VMEM((C_HEADS, T, T), f32),
                        pltpu.VMEM((C_HEADS, nb, T), f32), pltpu.VMEM((C_HEADS, T, W), bf16),
                        pltpu.VMEM((W, T), f32)],
        compiler_params=_params(("arbitrary", "arbitrary")),
        name="moba_prompt",
    )(tab, q, k, v)


def _layer_out_kernel(x_ref, ya_ref, ym_ref, yc_ref, sg_ref, g1_ref, sh2_ref, sc2_ref, g2_ref,
                      wa_ref, wm_ref, wc_ref, wo_ref, l1g_ref, l1b_ref, l2g_ref, l2b_ref,
                      wg_ref, wu_ref, wd_ref, o_ref, x1_ref, h2_ref, f_ref, *, alpha):
    kf = pl.program_id(2)
    d = D_MODEL

    @pl.when(kf == 0)
    def _():
        pa = _dot(ya_ref[0], wa_ref[...])
        pm = _dot(ym_ref[0], wm_ref[...])
        pc = _dot(yc_ref[0], wc_ref[...])
        pre = (sg_ref[0, :, :d].astype(f32) * pa + sg_ref[0, :, d:2 * d].astype(f32) * pm
               + sg_ref[0, :, 2 * d:].astype(f32) * pc)
        mix = _dot(pre.astype(wo_ref.dtype), wo_ref[...])
        x1 = _layer_norm(alpha * x_ref[0] + g1_ref[0] * mix, l1g_ref[...], l1b_ref[...])
        x1_ref[...] = x1
        h2_ref[...] = (x1 * (1.0 + sc2_ref[0]) + sh2_ref[0]).astype(h2_ref.dtype)
        f_ref[...] = jnp.zeros_like(f_ref)

    h2 = h2_ref[...]
    t = _silu(_dot(h2, wg_ref[...])) * _dot(h2, wu_ref[...])
    f_ref[...] += _dot(t.astype(wd_ref.dtype), wd_ref[...])

    @pl.when(kf == pl.num_programs(2) - 1)
    def _():
        o_ref[0] = _layer_norm(alpha * x1_ref[...] + g2_ref[0] * f_ref[...], l2g_ref[...], l2b_ref[...])


def layer_out(x, ya, ym, yc, sg, mods, wa, wm, wc, wo, l1g, l1b, l2g, l2b, wg, wu, wd, tm, tf, alpha):
    B, S, d = x.shape
    dff = wg.shape[1]
    tok = lambda a: pl.BlockSpec((1, tm, a.shape[2]), lambda b, i, k: (b, i, 0))
    full = lambda a: pl.BlockSpec(a.shape, lambda b, i, k: (0,) * a.ndim, pipeline_mode=pl.Buffered(1))
    return pl.pallas_call(
        functools.partial(_layer_out_kernel, alpha=alpha),
        grid=(B, S // tm, dff // tf),
        in_specs=[tok(x), tok(ya), tok(ym), tok(yc), tok(sg),
                  _mod_spec(mods, tm, 2), _mod_spec(mods, tm, 3), _mod_spec(mods, tm, 4), _mod_spec(mods, tm, 5),
                  full(wa), full(wm), full(wc), full(wo), full(l1g), full(l1b), full(l2g), full(l2b),
                  pl.BlockSpec((d, tf), lambda b, i, k: (0, k)),
                  pl.BlockSpec((d, tf), lambda b, i, k: (0, k)),
                  pl.BlockSpec((tf, d), lambda b, i, k: (k, 0))],
        out_specs=pl.BlockSpec((1, tm, d), lambda b, i, k: (b, i, 0)),
        out_shape=jax.ShapeDtypeStruct((B, S, d), f32),
        scratch_shapes=[pltpu.VMEM((tm, d), f32), pltpu.VMEM((tm, d), bf16), pltpu.VMEM((tm, d), f32)],
        compiler_params=_params(("arbitrary", "arbitrary", "arbitrary")),
        name="layer_out",
    )(x, ya, ym, yc, sg, mods, mods, mods, mods, wa, wm, wc, wo, l1g, l1b, l2g, l2b, wg, wu, wd)


def _absorb_kernel(qh_ref, wabs_ref, o_ref):
    for h in range(A_HEADS):
        blk = slice(h * HEAD_BLK, (h + 1) * HEAD_BLK)
        o_ref[:, blk] = _dot(qh_ref[:, blk], wabs_ref[h], HI)


def mla_absorb(qh, wabs):
    return pl.pallas_call(_absorb_kernel, out_shape=jax.ShapeDtypeStruct(qh.shape, f32),
                          compiler_params=_params(None), name="mla_absorb")(qh, wabs)


def _paged_kernel(pt_ref, qa_ref, qb_ref, latn_ref, krn_ref, qcol_ref, lat_hbm, krt_hbm, kt_hbm,
                  olat_ref, idx_ref, lat_buf, kr_buf, kt_buf, sems, m_s, l_s, acc_s, g_s,
                  *, layer, pages, cps):
    g = pl.program_id(0)
    c = g % cps
    slot = g % 2
    ppb = C_BLOCK // PAGE_SIZE
    blocks = pages // ppb

    def copies(step, sl):
        b = step // cps
        first = (step % cps) * pages
        out = []
        for j in range(pages):
            pg = pt_ref[b, first + j]
            out.append(pltpu.make_async_copy(lat_hbm.at[layer, pg], lat_buf.at[sl, j], sems.at[sl, 0]))
            out.append(pltpu.make_async_copy(krt_hbm.at[layer, pg], kr_buf.at[sl, j], sems.at[sl, 1]))
            out.append(pltpu.make_async_copy(kt_hbm.at[layer, pg], kt_buf.at[sl, j], sems.at[sl, 2]))
        return out

    @pl.when(g == 0)
    def _():
        for cp in copies(0, 0):
            cp.start()

    @pl.when(g + 1 < pl.num_programs(0))
    def _():
        for cp in copies(g + 1, 1 - slot):
            cp.start()

    qa = qa_ref[0]
    qr = qb_ref[0][:, A_NOPE:A_NOPE + A_ROPE]

    @pl.when(c == 0)
    def _():
        latn = latn_ref[0]
        s0 = jnp.sum(qa * latn, -1, keepdims=True) + jnp.sum(qr * krn_ref[0], -1, keepdims=True)
        m_s[...] = s0
        l_s[...] = jnp.ones_like(l_s)
        acc_s[...] = jnp.broadcast_to(latn, acc_s.shape)

    for cp in copies(g, slot):
        cp.wait()

    lat = lat_buf[slot].reshape(pages * PAGE_SIZE, A_KV_LORA).astype(bf16)
    krt = jnp.concatenate([kr_buf[slot, j] for j in range(pages)], axis=1).astype(bf16)
    s = _nt(qa.astype(bf16), lat) + _dot(qr.astype(bf16), krt)
    m_old = m_s[...]
    m_new = jnp.maximum(m_old, jnp.max(s, -1, keepdims=True))
    alpha = jnp.exp(m_old - m_new)
    p = jnp.exp(s - m_new)
    l_s[...] = alpha * l_s[...] + jnp.sum(p, -1, keepdims=True)
    acc_s[...] = alpha * acc_s[...] + _dot(p.astype(bf16), lat)
    m_s[...] = m_new

    qcb = jnp.broadcast_to(qcol_ref[0], (C_WIDTH, PAGE_SIZE))
    for jj in range(blocks):
        ks = kt_buf[slot, jj * ppb]
        for t in range(1, ppb):
            ks = ks + kt_buf[slot, jj * ppb + t]
        w = ks * qcb
        for h in range(C_HEADS):
            row = jnp.sum(w[h * C_DH:(h + 1) * C_DH, :], axis=0, keepdims=True)
            g_s[pl.ds(h * (cps * blocks) + c * blocks + jj, 1), :] = row

    @pl.when(c == cps - 1)
    def _():
        olat_ref[0] = acc_s[...] / l_s[...]
        nbp = cps * blocks
        gsum = _nt(jnp.ones((SUBLANES, PAGE_SIZE), f32), g_s[...], HI) * (1.0 / C_BLOCK)
        row = lax.broadcasted_iota(jnp.int32, gsum.shape, 0)
        lane_i = lax.broadcasted_iota(jnp.int32, gsum.shape, 1)
        lane = lane_i.astype(f32)
        gm = jnp.where(lane_i // nbp == row, gsum, -jnp.inf)
        out_lane = lax.broadcasted_iota(jnp.int32, (SUBLANES, LANES), 1)
        out = jnp.zeros((SUBLANES, LANES), f32)
        for r in range(C_TOPK):
            mx = jnp.max(gm, -1, keepdims=True)
            ix = jnp.min(jnp.where(gm == mx, lane, float(gm.shape[1])), -1, keepdims=True)
            out = jnp.where(out_lane == r, ix, out)
            gm = jnp.where(lane == ix, -jnp.inf, gm)
        head_off = (lax.broadcasted_iota(jnp.int32, (SUBLANES, LANES), 0) * nbp).astype(f32)
        idx_ref[0] = (out - head_off).astype(jnp.int32)


def sample_paged(page_table, qa, qb, latn, krn, qcol, cache_lat, cache_krt, cache_kt, layer, pages):
    DB, n_pages = page_table.shape
    cps = n_pages // pages
    nbp = n_pages * PAGE_SIZE // C_BLOCK
    seq = lambda a: pl.BlockSpec((1,) + a.shape[1:], lambda g, pt: (g // cps, 0, 0))
    hbm = pl.BlockSpec(memory_space=pl.ANY)
    grid_spec = pltpu.PrefetchScalarGridSpec(
        num_scalar_prefetch=1,
        grid=(DB * cps,),
        in_specs=[seq(qa), seq(qb), seq(latn), seq(krn), seq(qcol), hbm, hbm, hbm],
        out_specs=[pl.BlockSpec((1, A_HEADS, A_KV_LORA), lambda g, pt: (g // cps, 0, 0)),
                   pl.BlockSpec((1, SUBLANES, LANES), lambda g, pt: (g // cps, 0, 0))],
        scratch_shapes=[pltpu.VMEM((2, pages, PAGE_SIZE, A_KV_LORA), f32),
                        pltpu.VMEM((2, pages, A_ROPE, PAGE_SIZE), f32),
                        pltpu.VMEM((2, pages, C_WIDTH, PAGE_SIZE), f32),
                        pltpu.SemaphoreType.DMA((2, 3)),
                        pltpu.VMEM((A_HEADS, 1), f32), pltpu.VMEM((A_HEADS, 1), f32),
                        pltpu.VMEM((A_HEADS, A_KV_LORA), f32),
                        pltpu.VMEM((C_HEADS * nbp, PAGE_SIZE), f32)],
    )
    return pl.pallas_call(
        functools.partial(_paged_kernel, layer=layer, pages=pages, cps=cps),
        grid_spec=grid_spec,
        out_shape=[jax.ShapeDtypeStruct((DB, A_HEADS, A_KV_LORA), f32),
                   jax.ShapeDtypeStruct((DB, SUBLANES, LANES), jnp.int32)],
        compiler_params=_params(("arbitrary",)),
        name="sample_paged",
    )(page_table, qa, qb, latn, krn, qcol, cache_lat, cache_krt, cache_kt)


def _value_up_kernel(o_ref, w_ref, y_ref):
    y_ref[...] = _dot(o_ref[...], w_ref[...], HI).astype(y_ref.dtype)


def mla_value_up(olat, wuv_bd):
    return pl.pallas_call(_value_up_kernel,
                          out_shape=jax.ShapeDtypeStruct((olat.shape[0], wuv_bd.shape[1]), bf16),
                          compiler_params=_params(None), name="mla_value_up")(olat, wuv_bd)


def _mlstm_s1_kernel(zm_ref, cs_ref, cw_ref, cb_ref, wq_ref, wk_ref, wv_ref, bi_ref, bf_ref,
                     q_ref, k_ref, v_ref, i_ref, f_ref):
    W = M_WIDTH
    zm = zm_ref[...]
    u = zm[:, :W]
    acc = u * cw_ref[M_CONV - 1:M_CONV, :]
    for j in range(M_CONV - 1):
        acc = acc + cs_ref[:, j * W:(j + 1) * W] * cw_ref[j:j + 1, :]
    uc = _silu(acc + cb_ref[...])
    q_ref[...] = _dot(uc, wq_ref[...], HI)
    k_ref[...] = _dot(uc, wk_ref[...], HI)
    v_ref[...] = _dot(u, wv_ref[...], HI)
    i_ref[...] = zm[:, 2 * W:2 * W + LANES] + bi_ref[...]
    f_ref[...] = _log_sigmoid(zm[:, 2 * W + LANES:] + bf_ref[...])


def mlstm_sample_proj(zm, conv_state, cw, cb, wq, wk, wv, bi, bfg):
    DB = zm.shape[0]
    shapes = [jax.ShapeDtypeStruct((DB, M_WIDTH), f32)] * 3 + [jax.ShapeDtypeStruct((DB, LANES), f32)] * 2
    return pl.pallas_call(_mlstm_s1_kernel, out_shape=shapes, compiler_params=_params(None),
                          name="mlstm_sample_proj")(zm, conv_state, cw, cb, wq, wk, wv, bi, bfg)


def _mlstm_s2_kernel(qc_ref, kc_ref, kr_ref, vr_ref, or_ref, c0_ref, nc_ref, nr_ref, i_ref, f_ref, m_ref,
                     h_ref, c1_ref, n1_ref, m1_ref):
    qc = qc_ref[...]
    kc = kc_ref[...]
    kr = kr_ref[...]
    vr = vr_ref[...]
    c0 = c0_ref[...]
    i_pre = i_ref[...]
    a = f_ref[...] + m_ref[...]
    mt = jnp.maximum(a, i_pre)
    w_old = jnp.exp(a - mt)
    w_tok = jnp.exp(i_pre - mt)
    s = jnp.sum(qc * kc, axis=1, keepdims=True) * w_tok
    qn = jnp.sum(qc * nc_ref[...], axis=1, keepdims=True)
    qC = jnp.sum(qc * c0, axis=1, keepdims=True)
    num = w_old * qC + s * vr
    den = w_old * qn + s
    h = num / jnp.maximum(jnp.abs(den), jnp.exp(-mt))
    h_ref[...] = h * jax.nn.sigmoid(or_ref[...])
    c1_ref[...] = w_old * c0 + w_tok * (kc * vr)
    n1_ref[...] = w_old * nr_ref[...] + w_tok * kr
    m1_ref[...] = mt


def mlstm_sample_cell(qc, kc, kr, vr, orow, c0, nc, nr, i_pre, log_f, m0, rows):
    R = qc.shape[0]
    spec = lambda a: pl.BlockSpec((rows,) + a.shape[1:], lambda r: (r, 0, 0))
    ins = (qc, kc, kr, vr, orow, c0, nc, nr, i_pre, log_f, m0)
    shapes = [jax.ShapeDtypeStruct((R, 1, M_DH), f32), jax.ShapeDtypeStruct((R, M_DH, M_DH), f32),
              jax.ShapeDtypeStruct((R, 1, M_DH), f32), jax.ShapeDtypeStruct((R, 1, 1), f32)]
    return pl.pallas_call(
        _mlstm_s2_kernel,
        grid=(R // rows,),
        in_specs=[spec(a) for a in ins],
        out_specs=[spec(s) for s in shapes],
        out_shape=shapes,
        compiler_params=_params(("arbitrary",)),
        name="mlstm_sample_cell",
    )(*ins)


def _moba_decode_kernel(pt_ref, idx_ref, tab_ref, q_ref, kn_ref, vn_ref, *rest, n_sel, past):
    k_refs = rest[:n_sel]
    v_refs = rest[n_sel:2 * n_sel]
    o_ref = rest[2 * n_sel]
    b = pl.program_id(0)
    h = pl.program_id(1)
    ppb = C_BLOCK // PAGE_SIZE
    scale = C_DH ** -0.5
    q = q_ref[...]
    kt = jnp.concatenate([r[...] for r in k_refs], axis=1).astype(bf16)
    vt = jnp.concatenate([r[...] for r in v_refs], axis=1).astype(bf16)
    q8 = jnp.broadcast_to((q * scale).astype(bf16), (SUBLANES, C_DH))
    s = _dot(q8, kt)[:1, :]
    t = lax.broadcasted_iota(jnp.int32, (1, C_BLOCK), 1)
    bias = [_rel_bias(past - (idx_ref[(b * C_HEADS + h) * C_TOPK + j] * C_BLOCK + t), tab_ref, h)
            for j in range(n_sel // ppb)]
    s = s + jnp.concatenate(bias, axis=1)
    s_self = jnp.sum(q * kn_ref[...], -1, keepdims=True) * scale + tab_ref[h * REL_BUCKETS]
    m = jnp.maximum(jnp.max(s, -1, keepdims=True), s_self)
    p = jnp.exp(s - m)
    p_self = jnp.exp(s_self - m)
    l = jnp.sum(p, -1, keepdims=True) + p_self
    pv = _nt(jnp.broadcast_to(p.astype(bf16), (SUBLANES, p.shape[1])), vt)[:1, :]
    o_ref[...] = (pv + p_self * vn_ref[...]) / l


def moba_decode(page_table, idx_flat, tab, q, kn, vn, cache_kt, cache_vt, layer):
    DB, n_pages = page_table.shape
    ppb = C_BLOCK // PAGE_SIZE
    n_sel = C_TOPK * ppb
    past = n_pages * PAGE_SIZE

    def page_spec(j):
        def imap(b, h, pt, idx):
            blk = idx[(b * C_HEADS + h) * C_TOPK + j // ppb]
            return (layer, pt[b, blk * ppb + j % ppb], h, 0)
        return pl.BlockSpec((None, None, C_DH, PAGE_SIZE), imap)

    seq = pl.BlockSpec((None, None, 1, C_DH), lambda b, h, pt, idx: (b, h, 0, 0))
    grid_spec = pltpu.PrefetchScalarGridSpec(
        num_scalar_prefetch=2,
        grid=(DB, C_HEADS),
        in_specs=[pl.BlockSpec(memory_space=pltpu.SMEM), seq, seq, seq]
        + [page_spec(j) for j in range(n_sel)] + [page_spec(j) for j in range(n_sel)],
        out_specs=seq,
    )
    return pl.pallas_call(
        functools.partial(_moba_decode_kernel, n_sel=n_sel, past=past),
        grid_spec=grid_spec,
        out_shape=jax.ShapeDtypeStruct((DB, C_HEADS, 1, C_DH), f32),
        compiler_params=_params(("arbitrary", "arbitrary")),
        name="moba_decode",
    )(page_table, idx_flat, tab, q, kn, vn, *([cache_kt] * n_sel), *([cache_vt] * n_sel))


def _pack_w_in(w):
    d = w.shape[0]
    z = lambda n: jnp.zeros((d, n), w.dtype)
    o = 0
    parts = {}
    for name, size in (("cq", A_Q_LORA), ("ckv", A_KV_LORA), ("kr", A_ROPE), ("u", M_WIDTH), ("o", M_WIDTH),
                       ("i", M_HEADS), ("f", M_HEADS), ("qkv", 3 * C_WIDTH), ("g", 3 * D_MODEL)):
        parts[name] = w[:, o:o + size]
        o += size
    kr = parts["kr"]
    half = A_ROPE // 2
    pad = LANES - A_NOPE - A_ROPE
    krp = jnp.concatenate([z(A_NOPE), kr, z(pad)], axis=1)
    krs = jnp.concatenate([z(A_NOPE), kr[:, half:], kr[:, :half], z(pad)], axis=1)
    return jnp.concatenate([parts["cq"], parts["ckv"], krp, krs, parts["u"], parts["o"],
                            parts["i"], z(LANES - M_HEADS), parts["f"], z(LANES - M_HEADS),
                            parts["qkv"], parts["g"]], axis=1)


def _pack_w_q(w):
    r = w.shape[0]
    w3 = w.reshape(r, A_HEADS, A_NOPE + A_ROPE)
    half = A_ROPE // 2
    pad = jnp.zeros((r, A_HEADS, LANES - A_NOPE - A_ROPE), w.dtype)
    nope, rope = w3[..., :A_NOPE], w3[..., A_NOPE:]
    w1 = jnp.concatenate([nope, rope, pad], axis=-1)
    w2 = jnp.concatenate([jnp.zeros_like(nope), rope[..., half:], rope[..., :half], pad], axis=-1)
    return w1.reshape(r, -1), w2.reshape(r, -1)


def _rope_tables(pos):
    inv = ROPE_THETA ** (-jnp.arange(0, A_ROPE, 2, dtype=f32) / A_ROPE)
    ang = pos.astype(f32)[:, None] * inv[None, :]
    cos, sin = jnp.cos(ang), jnp.sin(ang)
    n = pos.shape[0]
    pad = jnp.zeros((n, LANES - A_NOPE - A_ROPE), f32)
    ct = jnp.concatenate([jnp.ones((n, A_NOPE), f32), cos, cos, pad], axis=1)
    st = jnp.concatenate([jnp.zeros((n, A_NOPE), f32), -sin, sin, pad], axis=1)
    return ct, st


def _block_diag(w):
    hN, a, b = w.shape
    eye = jnp.eye(hN, dtype=w.dtype)
    return (eye[:, None, :, None] * w[:, :, None, :]).reshape(hN * a, hN * b)


def _lane_pad(v, width=LANES):
    return jnp.pad(v, ((0, 0), (0, width - v.shape[-1])))


def kernel(x_prompt, x_sample, cache_mla_latent, cache_mla_krope, cache_moba_k, cache_moba_v, state_mlstm_C, state_mlstm_n, state_mlstm_m, state_mlstm_conv, page_table, c_prompt, c_sample, rel_table, w_ada, b_ada, w_in, g_q_norm, w_q_up, g_kv_norm, w_kv_up, conv_w, conv_b, w_mq, w_mk, w_mv, b_i, b_f, w_br_a, w_br_m, w_br_c, w_out, ln1_g, ln1_b, w_ff_gate, w_ff_up, w_ff_down, ln2_g, ln2_b):
    B, S, d = x_prompt.shape
    DB, T, _ = x_sample.shape
    depth = w_in.shape[0]
    n_pool = cache_mla_latent.shape[1]
    n_pages = page_table.shape[1]
    past = n_pages * PAGE_SIZE
    assert T == 1 and past % C_BLOCK == 0 and past // C_BLOCK >= C_TOPK and S % C_BLOCK == 0
    alpha = (2 * depth) ** 0.25
    dff = w_ff_gate.shape[2]

    mods_all = ada_mods(jnp.concatenate([c_prompt, c_sample], axis=0), w_ada, b_ada)
    ct_p, st_p = _rope_tables(jnp.arange(S, dtype=jnp.int32))
    ct_s, st_s = _rope_tables(jnp.full((DB,), past, jnp.int32))
    tab = rel_table.T.reshape(-1)
    cache_kt = jnp.transpose(cache_moba_k, (0, 1, 3, 4, 2)).reshape(depth, n_pool, C_WIDTH, PAGE_SIZE)
    cache_vt = jnp.transpose(cache_moba_v, (0, 1, 3, 4, 2)).reshape(depth, n_pool, C_WIDTH, PAGE_SIZE)
    cache_krt = jnp.transpose(cache_mla_krope, (0, 1, 3, 2))
    xs = x_sample.reshape(1, DB, d)
    xp = x_prompt

    new_p = [[] for _ in range(8)]
    new_s = [[] for _ in range(8)]
    for l in range(depth):
        w_pack = _pack_w_in(w_in[l]).astype(bf16)
        wq1, wq2 = _pack_w_q(w_q_up[l])
        w_uk = w_kv_up[l][..., :A_NOPE]
        w_uv = w_kv_up[l][..., A_NOPE:]
        wuk = jnp.concatenate([w_uk, jnp.zeros_like(w_uk)], axis=-1).reshape(A_KV_LORA, -1)
        wuv = w_uv.reshape(A_KV_LORA, -1)
        gq = g_q_norm[l][None, :]
        gkv = g_kv_norm[l][None, :]
        wq_bd = _block_diag(w_mq[l])
        wk_bd = _block_diag(w_mk[l]) * (M_DH ** -0.5)
        wv_bd = _block_diag(w_mv[l])
        bi = _lane_pad(b_i[l][None, :])
        bfg = _lane_pad(b_f[l][None, :])
        cw = conv_w[l]
        cb = conv_b[l][None, :]
        tail_w = (w_br_a[l].astype(bf16), w_br_m[l].astype(bf16), w_br_c[l].astype(bf16), w_out[l].astype(bf16),
                  ln1_g[l][None, :], ln1_b[l][None, :], ln2_g[l][None, :], ln2_b[l][None, :],
                  w_ff_gate[l].astype(bf16), w_ff_up[l].astype(bf16), w_ff_down[l].astype(bf16))

        mods_p = mods_all[l, :B].reshape(B, 1, 6 * d)
        za, zm, qc, kc, vc, sg = in_proj(xp, mods_p, w_pack, tm=512)
        qh, kh, vh, lat, krope = mla_prep(za, ct_p, st_p, gq, gkv, wq1.astype(bf16), wq2.astype(bf16),
                                          wuk.astype(bf16), wuv.astype(bf16), tm=512, qk_dtype=bf16)
        ya = mla_attn(qh, kh, vh, tq=512)
        ym, c_st, n_st, m_st = mlstm_prompt(zm, cw, cb, wq_bd.astype(bf16), wk_bd.astype(bf16),
                                            wk_bd.T.astype(bf16), wv_bd.astype(bf16), bi, bfg, L=256)
        yc = moba_prompt(qc, kc, vc, tab)
        xp_new = layer_out(xp, ya, ym, yc, sg, mods_p, *tail_w, tm=512, tf=dff // 2, alpha=alpha)
        C1 = jnp.stack([c_st[:, h * M_DH:(h + 1) * M_DH, h * M_DH:(h + 1) * M_DH] for h in range(M_HEADS)], axis=1)
        n1 = jnp.stack([n_st[:, h * M_DH:(h + 1) * M_DH, h] for h in range(M_HEADS)], axis=1)
        m1 = m_st[:, :M_HEADS, 0]
        u_p = zm[:, :, :M_WIDTH]
        conv_p = jnp.pad(u_p, ((0, 0), (M_CONV - 1, 0), (0, 0)))[:, -(M_CONV - 1):]
        for lst, val in zip(new_p, (lat, krope, kc.reshape(B, S, C_HEADS, C_DH), vc.reshape(B, S, C_HEADS, C_DH),
                                    C1, n1, m1, conv_p)):
            lst.append(val)
        xp = xp_new

        mods_s = mods_all[l, B:].reshape(1, DB, 6 * d)
        za, zm, qc, kc, vc, sg = in_proj(xs, mods_s, w_pack, tm=DB)
        qh, _, _, lat, krope = mla_prep(za, ct_s, st_s, gq, gkv, wq1, wq2, wuk, wuv, tm=DB, qk_dtype=f32)
        wabs = jnp.concatenate([jnp.transpose(w_uk, (1, 2, 0)),
                                jnp.zeros((A_HEADS, HEAD_BLK - A_NOPE, A_KV_LORA), f32)], axis=1)
        qh2 = qh.reshape(DB, A_HEADS * HEAD_BLK)
        qabs = mla_absorb(qh2, wabs)
        olat, idx = sample_paged(page_table, qabs.reshape(DB, A_HEADS, A_KV_LORA), qh2.reshape(DB, A_HEADS, HEAD_BLK),
                                 lat.reshape(DB, 1, A_KV_LORA), krope.reshape(DB, 1, A_ROPE),
                                 qc.reshape(DB, C_WIDTH, 1), cache_mla_latent, cache_krt, cache_kt, l, pages=32)
        wuv_bd = _block_diag(jnp.transpose(w_uv, (1, 0, 2)))
        ya = mla_value_up(olat.reshape(DB, A_HEADS * A_KV_LORA), wuv_bd)

        zm2 = zm.reshape(DB, ZM_W)
        conv_state = state_mlstm_conv[l]
        q_m, k_m, v_m, i_pre, log_f = mlstm_sample_proj(zm2, conv_state.reshape(DB, -1), cw, cb,
                                                        wq_bd, wk_bd, wv_bd, bi, bfg)
        R = DB * M_HEADS
        col = lambda a: a.reshape(R, M_DH, 1)
        rowv = lambda a: a.reshape(R, 1, M_DH)
        sc = lambda a: a.reshape(R, 1, 1)
        h_m, C1, n1, m1 = mlstm_sample_cell(
            col(q_m), col(k_m), rowv(k_m), rowv(v_m), rowv(zm2[:, M_WIDTH:2 * M_WIDTH]),
            state_mlstm_C[l].reshape(R, M_DH, M_DH), col(state_mlstm_n[l]), rowv(state_mlstm_n[l]),
            sc(i_pre[:, :M_HEADS]), sc(log_f[:, :M_HEADS]), sc(state_mlstm_m[l]), rows=32)
        ym = h_m.reshape(DB, M_WIDTH).astype(bf16)
        u_s = zm2[:, :M_WIDTH]
        conv_s = jnp.concatenate([conv_state[:, 1:], u_s[:, None, :]], axis=1)

        idx_flat = idx[:, :C_HEADS, :C_TOPK].reshape(-1)
        per_head = lambda a: a.reshape(DB, C_HEADS, 1, C_DH)
        yc = moba_decode(page_table, idx_flat, tab, per_head(qc), per_head(kc), per_head(vc), cache_kt, cache_vt, l)
        xs_new = layer_out(xs, ya.reshape(1, DB, -1), ym.reshape(1, DB, -1), yc.reshape(1, DB, -1).astype(bf16),
                           sg, mods_s, *tail_w, tm=DB, tf=dff // 2, alpha=alpha)
        for lst, val in zip(new_s, (lat.reshape(DB, 1, A_KV_LORA), krope.reshape(DB, 1, A_ROPE),
                                    kc.reshape(DB, 1, C_HEADS, C_DH), vc.reshape(DB, 1, C_HEADS, C_DH),
                                    C1.reshape(DB, M_HEADS, M_DH, M_DH), n1.reshape(DB, M_HEADS, M_DH),
                                    m1.reshape(DB, M_HEADS), conv_s)):
            lst.append(val)
        xs = xs_new

    outs_p = [jnp.stack(a) for a in new_p]
    outs_s = [jnp.stack(a) for a in new_s]
    return (xp, xs.reshape(DB, T, d), *outs_p, *outs_s)
```

```python
import functools
import math

import jax
import jax.numpy as jnp
from jax import lax
from jax.experimental import pallas as pl
from jax.experimental.pallas import tpu as pltpu

f32 = jnp.float32
bf16 = jnp.bfloat16
HI = lax.Precision.HIGHEST

D_MODEL = 1024
PAGE_SIZE = 128
A_HEADS = 8
A_NOPE = 64
A_ROPE = 32
A_VDIM = 64
A_Q_LORA = 256
A_KV_LORA = 128
ROPE_THETA = 10000.0
M_HEADS = 4
M_DH = 64
M_WIDTH = M_HEADS * M_DH
M_CONV = 4
C_HEADS = 4
C_DH = 64
C_WIDTH = C_HEADS * C_DH
C_BLOCK = 256
C_TOPK = 3
REL_BUCKETS = 32
REL_MAX_DIST = 128
LN_EPS = 1e-5
RMS_EPS = 1e-6
LANES = 128
SUBLANES = 8
VMEM_LIMIT = 56 * 1024 * 1024
NEG = -1e30

ZA_W = A_Q_LORA + A_KV_LORA + 2 * LANES
ZM_W = 2 * M_WIDTH + 2 * LANES
ZC_W = 3 * C_WIDTH
ZG_W = 3 * D_MODEL
PACK_W = ZA_W + ZM_W + ZC_W + ZG_W
HEAD_BLK = LANES


def _rel_thresholds():
    exact = REL_BUCKETS // 2
    out = []
    for k in range(1, REL_BUCKETS - exact):
        n = exact
        while int(math.log(n / exact) / math.log(REL_MAX_DIST / exact) * (REL_BUCKETS - exact)) < k:
            n += 1
        out.append(n)
    return tuple(out)


REL_THRESH = _rel_thresholds()


def _nt(a, b, precision=None):
    return lax.dot_general(a, b, (((1,), (1,)), ((), ())), precision=precision, preferred_element_type=f32)


def _tn(a, b):
    return lax.dot_general(a, b, (((0,), (0,)), ((), ())), preferred_element_type=f32)


def _dot(a, b, precision=None):
    return jnp.dot(a, b, precision=precision, preferred_element_type=f32)


def _silu(x):
    return x * jax.nn.sigmoid(x)


def _layer_norm(x, g, b):
    mu = jnp.mean(x, -1, keepdims=True)
    xc = x - mu
    var = jnp.mean(xc * xc, -1, keepdims=True)
    return xc * lax.rsqrt(var + LN_EPS) * g + b


def _rms_norm(x, g):
    return x * lax.rsqrt(jnp.mean(x * x, -1, keepdims=True) + RMS_EPS) * g


def _rel_bias(dist, tab_ref, h):
    n = jnp.maximum(dist, 0)
    exact = REL_BUCKETS // 2
    large = jnp.full(n.shape, exact, jnp.int32)
    for t in REL_THRESH:
        large = large + (n >= t).astype(jnp.int32)
    bucket = jnp.where(n < exact, n, large)
    val = jnp.full(n.shape, tab_ref[h * REL_BUCKETS + REL_BUCKETS - 1], f32)
    for b in range(REL_BUCKETS - 1):
        val = jnp.where(bucket == b, tab_ref[h * REL_BUCKETS + b], val)
    return val


def _params(sem):
    return pltpu.CompilerParams(dimension_semantics=sem, vmem_limit_bytes=VMEM_LIMIT)


def _ada_kernel(c_ref, w_ref, b_ref, o_ref):
    c = c_ref[...]
    o_ref[...] = _dot(_silu(c), w_ref[...], HI) + b_ref[...]


def ada_mods(c_all, w_ada, b_ada):
    depth, d, n = w_ada.shape
    rows = c_all.shape[0]
    tn = n // 4
    return pl.pallas_call(
        _ada_kernel,
        grid=(depth, n // tn),
        in_specs=[
            pl.BlockSpec((rows, d), lambda l, j: (0, 0)),
            pl.BlockSpec((None, d, tn), lambda l, j: (l, 0, j)),
            pl.BlockSpec((None, 1, tn), lambda l, j: (l, 0, j)),
        ],
        out_specs=pl.BlockSpec((None, rows, tn), lambda l, j: (l, 0, j)),
        out_shape=jax.ShapeDtypeStruct((depth, rows, n), f32),
        compiler_params=_params(("arbitrary", "arbitrary")),
        name="ada_mods",
    )(c_all, w_ada, b_ada.reshape(depth, 1, n))


def _mod_spec(mods, tm, col):
    per_row = mods.shape[1] != 1
    rows = tm if per_row else 1
    return pl.BlockSpec((1, rows, D_MODEL), lambda b, i, *_: (b, i if per_row else 0, col))


def _in_proj_kernel(x_ref, sh_ref, sc_ref, w_ref, za_ref, zm_ref, q_ref, k_ref, v_ref, sg_ref):
    h = (x_ref[0] * (1.0 + sc_ref[0]) + sh_ref[0]).astype(w_ref.dtype)

    def mm(c0, c1):
        return _dot(h, w_ref[:, c0:c1])

    c = 0
    za_ref[0] = mm(c, c + ZA_W)
    c += ZA_W
    zm_ref[0] = mm(c, c + ZM_W)
    c += ZM_W
    q_ref[0] = mm(c, c + C_WIDTH)
    k_ref[0] = mm(c + C_WIDTH, c + 2 * C_WIDTH)
    v_ref[0] = mm(c + 2 * C_WIDTH, c + 3 * C_WIDTH)
    c += ZC_W
    sg_ref[0] = jax.nn.sigmoid(mm(c, c + ZG_W)).astype(sg_ref.dtype)


def in_proj(x, mods, w_pack, tm):
    B, S, d = x.shape
    widths = (ZA_W, ZM_W, C_WIDTH, C_WIDTH, C_WIDTH, ZG_W)
    dts = (f32, f32, f32, f32, f32, bf16)
    return pl.pallas_call(
        _in_proj_kernel,
        grid=(B, S // tm),
        in_specs=[
            pl.BlockSpec((1, tm, d), lambda b, i: (b, i, 0)),
            _mod_spec(mods, tm, 0),
            _mod_spec(mods, tm, 1),
            pl.BlockSpec((d, PACK_W), lambda b, i: (0, 0)),
        ],
        out_specs=[pl.BlockSpec((1, tm, w), lambda b, i: (b, i, 0)) for w in widths],
        out_shape=[jax.ShapeDtypeStruct((B, S, w), t) for w, t in zip(widths, dts)],
        compiler_params=_params(("arbitrary", "arbitrary")),
        name="in_proj",
    )(x, mods, mods, w_pack)


def _mla_prep_kernel(za_ref, ct_ref, st_ref, gq_ref, gkv_ref, wq1_ref, wq2_ref, wuk_ref, wuv_ref,
                     qh_ref, kh_ref, vh_ref, lat_ref, kr_ref, *, prec):
    za = za_ref[0]
    cq = za[:, :A_Q_LORA]
    ckv = za[:, A_Q_LORA:A_Q_LORA + A_KV_LORA]
    krp = za[:, A_Q_LORA + A_KV_LORA:A_Q_LORA + A_KV_LORA + LANES]
    krs = za[:, A_Q_LORA + A_KV_LORA + LANES:]
    ct = ct_ref[...]
    st = st_ref[...]
    scale = (A_NOPE + A_ROPE) ** -0.5
    cqn = _rms_norm(cq, gq_ref[...]).astype(wq1_ref.dtype)
    q1 = _dot(cqn, wq1_ref[...], prec)
    q2 = _dot(cqn, wq2_ref[...], prec)
    lat = _rms_norm(ckv, gkv_ref[...])
    lat_ref[0] = lat
    kro = krp * ct + krs * st
    kr_ref[0] = kro[:, A_NOPE:A_NOPE + A_ROPE]
    latc = lat.astype(wuk_ref.dtype)
    kn = _dot(latc, wuk_ref[...], prec)
    for h in range(A_HEADS):
        blk = slice(h * HEAD_BLK, (h + 1) * HEAD_BLK)
        qh_ref[0, :, blk] = ((q1[:, blk] * ct + q2[:, blk] * st) * scale).astype(qh_ref.dtype)
        kh_ref[0, :, blk] = (kn[:, blk] + kro).astype(kh_ref.dtype)
    vh_ref[0] = _dot(latc, wuv_ref[...], prec).astype(vh_ref.dtype)


def mla_prep(za, ctab, stab, gq, gkv, wq1, wq2, wuk, wuv, tm, qk_dtype):
    B, S, _ = za.shape
    full = lambda a: pl.BlockSpec(a.shape, lambda b, i: (0,) * a.ndim)
    tok = lambda w: pl.BlockSpec((1, tm, w), lambda b, i: (b, i, 0))
    widths = (A_HEADS * HEAD_BLK, A_HEADS * HEAD_BLK, A_HEADS * A_VDIM, A_KV_LORA, A_ROPE)
    dts = (qk_dtype, qk_dtype, qk_dtype, f32, f32)
    prec = HI if qk_dtype == f32 else None
    return pl.pallas_call(
        functools.partial(_mla_prep_kernel, prec=prec),
        grid=(B, S // tm),
        in_specs=[tok(ZA_W),
                  pl.BlockSpec((tm, LANES), lambda b, i: (i, 0)),
                  pl.BlockSpec((tm, LANES), lambda b, i: (i, 0)),
                  full(gq), full(gkv), full(wq1), full(wq2), full(wuk), full(wuv)],
        out_specs=[tok(w) for w in widths],
        out_shape=[jax.ShapeDtypeStruct((B, S, w), t) for w, t in zip(widths, dts)],
        compiler_params=_params(("arbitrary", "arbitrary")),
        name="mla_prep",
    )(za, ctab, stab, gq, gkv, wq1, wq2, wuk, wuv)


def _mla_attn_kernel(q_ref, k_ref, v_ref, o_ref, *, tq):
    S = q_ref.shape[1]
    lane = lax.broadcasted_iota(jnp.int32, (tq, 2 * A_VDIM), 1)
    for t in range(S // tq):
        kv_len = (t + 1) * tq
        row = t * tq + lax.broadcasted_iota(jnp.int32, (tq, kv_len), 0)
        col = lax.broadcasted_iota(jnp.int32, (tq, kv_len), 1)
        causal = col <= row
        vpair = v_ref[0, :kv_len, :]
        outs = []
        for hh in range(2):
            blk = slice(hh * HEAD_BLK, (hh + 1) * HEAD_BLK)
            s = _nt(q_ref[0, t * tq:(t + 1) * tq, blk], k_ref[0, :kv_len, blk])
            s = jnp.where(causal, s, -jnp.inf)
            m = jnp.max(s, -1, keepdims=True)
            p = jnp.exp(s - m)
            l = jnp.sum(p, -1, keepdims=True)
            outs.append(_dot(p.astype(vpair.dtype), vpair) / l)
        o_ref[0, t * tq:(t + 1) * tq, :] = jnp.where(lane < A_VDIM, outs[0], outs[1]).astype(o_ref.dtype)


def mla_attn(qh, kh, vh, tq):
    B, S, _ = qh.shape
    pairs = A_HEADS // 2
    return pl.pallas_call(
        functools.partial(_mla_attn_kernel, tq=tq),
        grid=(B, pairs),
        in_specs=[pl.BlockSpec((1, S, 2 * HEAD_BLK), lambda b, p: (b, 0, p)),
                  pl.BlockSpec((1, S, 2 * HEAD_BLK), lambda b, p: (b, 0, p)),
                  pl.BlockSpec((1, S, 2 * A_VDIM), lambda b, p: (b, 0, p))],
        out_specs=pl.BlockSpec((1, S, 2 * A_VDIM), lambda b, p: (b, 0, p)),
        out_shape=jax.ShapeDtypeStruct((B, S, A_HEADS * A_VDIM), bf16),
        compiler_params=_params(("arbitrary", "arbitrary")),
        name="mla_attn",
    )(qh, kh, vh)


def _log_sigmoid(x):
    return jnp.minimum(x, 0.0) - jnp.log1p(jnp.exp(-jnp.abs(x)))


def _mlstm_kernel(zm_ref, cw_ref, cb_ref, wq_ref, wk_ref, wkt_ref, wv_ref, bi_ref, bf_ref,
                  ym_ref, c_out_ref, n_out_ref, m_out_ref,
                  c_st, n_st, m_row, m_col, tail):
    L = zm_ref.shape[1]
    W = M_WIDTH
    c = pl.program_id(1)

    @pl.when(c == 0)
    def _():
        c_st[...] = jnp.zeros_like(c_st)
        n_st[...] = jnp.zeros_like(n_st)
        m_row[...] = jnp.zeros_like(m_row)
        m_col[...] = jnp.zeros_like(m_col)
        tail[...] = jnp.zeros_like(tail)

    zm = zm_ref[0]
    u = zm[:, :W]
    o_raw = zm[:, W:2 * W]
    ig = zm[:, 2 * W:2 * W + LANES]
    fg = zm[:, 2 * W + LANES:]

    rows = lax.broadcasted_iota(jnp.int32, (L, W), 0)
    tl = tail[...]
    acc = u * cw_ref[M_CONV - 1:M_CONV, :]
    for j in range(1, M_CONV):
        ru = pltpu.roll(u, j, 0)
        rt = jnp.concatenate([pltpu.roll(tl, j, 0), jnp.zeros((L - SUBLANES, W), f32)], axis=0)
        acc = acc + jnp.where(rows < j, rt, ru) * cw_ref[M_CONV - 1 - j:M_CONV - j, :]
    tail[...] = u[L - SUBLANES:, :]
    uc = _silu(acc + cb_ref[...]).astype(bf16)
    ub = u.astype(bf16)

    q = _dot(uc, wq_ref[...])
    k = _dot(uc, wk_ref[...])
    kt = _nt(wkt_ref[...], uc)
    v = _dot(ub, wv_ref[...])
    qb = q.astype(bf16)
    kb = k.astype(bf16)
    vb = v.astype(bf16)

    i_c = ig + bi_ref[...]
    f_c = _log_sigmoid(fg + bf_ref[...])
    t_i = lax.broadcasted_iota(jnp.int32, (L, L), 0)
    s_i = lax.broadcasted_iota(jnp.int32, (L, L), 1)
    causal = s_i <= t_i
    tri = causal.astype(f32)
    b_c = _dot(tri, f_c, HI)
    b_r = b_c.T[:SUBLANES, :]
    i_r = i_c.T[:SUBLANES, :]
    a_c = b_c + m_row[...]

    inter = _dot(qb, c_st[...].astype(bf16))
    qn = _dot(qb, n_st[...].astype(bf16))
    lane_head = lax.broadcasted_iota(jnp.int32, (1, W), 1) // M_DH
    intra = jnp.zeros((L, W), f32)
    w_full = jnp.zeros((L, W), f32)
    r_full = jnp.zeros((L, W), f32)
    for h in range(M_HEADS):
        hm = lane_head == h
        dmat = jnp.where(causal, b_c[:, h:h + 1] - b_r[h:h + 1, :] + i_r[h:h + 1, :], -jnp.inf)
        a_h = a_c[:, h:h + 1]
        mt = jnp.maximum(a_h, jnp.max(dmat, -1, keepdims=True))
        pm = jnp.exp(dmat - mt)
        s = _nt(jnp.where(hm, qb, jnp.zeros_like(qb)), kb) * pm
        w_h = jnp.exp(a_h - mt)
        den = w_h * qn[:, h:h + 1] + jnp.sum(s, -1, keepdims=True)
        r_h = 1.0 / jnp.maximum(jnp.abs(den), jnp.exp(-mt))
        intra = intra + _dot(s.astype(bf16), jnp.where(hm, vb, jnp.zeros_like(vb)))
        w_full = jnp.where(hm, w_h, w_full)
        r_full = jnp.where(hm, r_h, r_full)
    hout = (w_full * inter + intra) * r_full
    ym_ref[0] = (hout * jax.nn.sigmoid(o_raw)).astype(ym_ref.dtype)

    b_end = b_r[:, L - 1:L]
    g_r = b_end - b_r + i_r
    mc = m_col[...][:, :1]
    m_new = jnp.maximum(b_end + mc, jnp.max(g_r, -1, keepdims=True))
    w_old = jnp.exp(b_end + mc - m_new)
    w_tok = jnp.exp(g_r - m_new)
    w_tok_full = jnp.concatenate([jnp.broadcast_to(w_tok[h:h + 1, :], (M_DH, L)) for h in range(M_HEADS)], axis=0)
    w_old_full = jnp.concatenate([jnp.broadcast_to(w_old[h:h + 1, :], (M_DH, 1)) for h in range(M_HEADS)], axis=0)
    ktw = (kt * w_tok_full).astype(bf16)
    row_head = lax.broadcasted_iota(jnp.int32, (W, W), 0) // M_DH
    col_head = lax.broadcasted_iota(jnp.int32, (W, W), 1) // M_DH
    c_st[...] = w_old_full * c_st[...] + jnp.where(row_head == col_head, _dot(ktw, vb), 0.0)
    n_rows = lax.broadcasted_iota(jnp.int32, (W, LANES), 0) // M_DH
    n_cols = lax.broadcasted_iota(jnp.int32, (W, LANES), 1)
    ksum = _dot(ktw, jnp.ones((L, LANES), bf16))
    n_st[...] = w_old_full * n_st[...] + jnp.where(n_rows == n_cols, ksum, 0.0)
    eye = (lax.broadcasted_iota(jnp.int32, (SUBLANES, LANES), 0)
           == lax.broadcasted_iota(jnp.int32, (SUBLANES, LANES), 1)).astype(f32)
    m_row[...] = jnp.sum(eye * m_new, axis=0, keepdims=True)
    m_col[...] = jnp.broadcast_to(m_new, (SUBLANES, LANES))

    @pl.when(c == pl.num_programs(1) - 1)
    def _():
        c_out_ref[0] = c_st[...]
        n_out_ref[0] = n_st[...]
        m_out_ref[0] = m_col[...]


def mlstm_prompt(zm, cw, cb, wq, wk, wkt, wv, bi, bfg, L):
    B, S, _ = zm.shape
    W = M_WIDTH
    full = lambda a: pl.BlockSpec(a.shape, lambda b, c: (0,) * a.ndim)
    return pl.pallas_call(
        _mlstm_kernel,
        grid=(B, S // L),
        in_specs=[pl.BlockSpec((1, L, ZM_W), lambda b, c: (b, c, 0)),
                  full(cw), full(cb), full(wq), full(wk), full(wkt), full(wv), full(bi), full(bfg)],
        out_specs=[pl.BlockSpec((1, L, W), lambda b, c: (b, c, 0)),
                   pl.BlockSpec((1, W, W), lambda b, c: (b, 0, 0)),
                   pl.BlockSpec((1, W, LANES), lambda b, c: (b, 0, 0)),
                   pl.BlockSpec((1, SUBLANES, LANES), lambda b, c: (b, 0, 0))],
        out_shape=[jax.ShapeDtypeStruct((B, S, W), bf16),
                   jax.ShapeDtypeStruct((B, W, W), f32),
                   jax.ShapeDtypeStruct((B, W, LANES), f32),
                   jax.ShapeDtypeStruct((B, SUBLANES, LANES), f32)],
        scratch_shapes=[pltpu.VMEM((W, W), f32), pltpu.VMEM((W, LANES), f32),
                        pltpu.VMEM((1, LANES), f32), pltpu.VMEM((SUBLANES, LANES), f32),
                        pltpu.VMEM((SUBLANES, W), f32)],
        compiler_params=_params(("arbitrary", "arbitrary")),
        name="mlstm_prompt",
    )(zm, cw, cb, wq, wk, wkt, wv, bi, bfg)


def _moba_prompt_kernel(tab_ref, q_ref, k_ref, v_ref, o_ref,
                        k16, vt16, kmean, b_own, b_prev, sel_ref, qm_ref, acc_ref):
    S = k_ref.shape[1]
    nb = S // C_BLOCK
    T = C_BLOCK
    H = C_HEADS
    i = pl.program_id(1)
    key_i = lax.broadcasted_iota(jnp.int32, (T, T), 0)
    qry_i = lax.broadcasted_iota(jnp.int32, (T, T), 1)

    @pl.when(i == 0)
    def _():
        kf = k_ref[0]
        for j in range(nb):
            k16[j] = kf[j * T:(j + 1) * T, :].astype(bf16)
            vt16[j] = v_ref[0, j * T:(j + 1) * T, :].T.astype(bf16)
        kmean[...] = jnp.sum(kf.reshape(nb, T, C_WIDTH), axis=1) * (1.0 / T)
        for h in range(H):
            b_own[h] = _rel_bias(qry_i - key_i, tab_ref, h)
            b_prev[h] = _rel_bias(qry_i - key_i + T, tab_ref, h)

    qf = q_ref[0]
    lane_head = lax.broadcasted_iota(jnp.int32, (1, C_WIDTH), 1) // C_DH
    jio = lax.broadcasted_iota(jnp.int32, (nb, T), 0)
    valid = jio < i
    scale = C_DH ** -0.5
    for h in range(H):
        hm = lane_head == h
        qm_ref[h] = (jnp.where(hm, qf, 0.0) * scale).astype(bf16)
        gs = _nt(jnp.where(hm, kmean[...], 0.0), qf, HI)
        sel = jnp.zeros((nb, T), f32)
        for j in range(nb):
            vj = gs[j:j + 1, :]
            beats = jnp.where(gs > vj, 1.0, jnp.where((gs == vj) & (jio < j), 1.0, 0.0))
            cnt = jnp.sum(jnp.where(valid, beats, 0.0), axis=0, keepdims=True)
            sel = jnp.where(jio == j, jnp.where(cnt < C_TOPK, 1.0, 0.0), sel)
        sel_ref[h] = sel

    rows = lambda h: slice(h * C_DH, (h + 1) * C_DH)

    kblk = k16[i]
    vblk = vt16[i]
    ms, ls = [], []
    for h in range(H):
        s = _nt(kblk, qm_ref[h]) + b_own[h]
        s = jnp.where(key_i <= qry_i, s, NEG)
        m0 = jnp.max(s, axis=0, keepdims=True)
        p = jnp.exp(s - m0)
        ms.append(m0)
        ls.append(jnp.sum(p, axis=0, keepdims=True))
        acc_ref[rows(h), :] = _dot(vblk[rows(h), :], p.astype(bf16))

    def past_block(j, bias_of, carry):
        ms, ls = carry
        kblk = k16[j]
        vblk = vt16[j]
        ms_new, ls_new = [], []
        for h in range(H):
            sj = _nt(kblk, qm_ref[h]) + bias_of(h)
            sj = jnp.where(sel_ref[h, pl.ds(j, 1), :] > 0.5, sj, NEG)
            m_new = jnp.maximum(ms[h], jnp.max(sj, axis=0, keepdims=True))
            alpha = jnp.exp(ms[h] - m_new)
            pj = jnp.exp(sj - m_new)
            ms_new.append(m_new)
            ls_new.append(alpha * ls[h] + jnp.sum(pj, axis=0, keepdims=True))
            acc_ref[rows(h), :] = alpha * acc_ref[rows(h), :] + _dot(vblk[rows(h), :], pj.astype(bf16))
        return tuple(ms_new), tuple(ls_new)

    carry = (tuple(ms), tuple(ls))
    carry = lax.cond(i >= 1, lambda c: past_block(i - 1, lambda h: b_prev[h], c), lambda c: c, carry)
    ms, ls = lax.fori_loop(
        0, jnp.maximum(i - 1, 0),
        lambda j, c: past_block(j, lambda h: tab_ref[h * REL_BUCKETS + REL_BUCKETS - 1], c), carry)
    res = jnp.concatenate([acc_ref[rows(h), :] / ls[h] for h in range(H)], axis=0)
    o_ref[0] = res.T.astype(o_ref.dtype)


def moba_prompt(q, k, v, tab):
    B, S, W = q.shape
    nb = S // C_BLOCK
    T = C_BLOCK
    return pl.pallas_call(
        _moba_prompt_kernel,
        grid=(B, nb),
        in_specs=[pl.BlockSpec(memory_space=pltpu.SMEM),
                  pl.BlockSpec((1, T, W), lambda b, i: (b, i, 0)),
                  pl.BlockSpec((1, S, W), lambda b, i: (b, 0, 0)),
                  pl.BlockSpec((1, S, W), lambda b, i: (b, 0, 0))],
        out_specs=pl.BlockSpec((1, T, W), lambda b, i: (b, i, 0)),
        out_shape=jax.ShapeDtypeStruct((B, S, W), bf16),
        scratch_shapes=[pltpu.VMEM((nb, T, W), bf16), pltpu.VMEM((nb, W, T), bf16), pltpu.VMEM((nb, W), f32),
                        pltpu.VMEM((C_HEADS, T, T), f32), pltpu.VMEM((C_HEADS, T, T), f32),
                        pltpu.VMEM((C_HEADS, nb, T), f32), pltpu.VMEM((C_HEADS, T, W), bf16),
                        pltpu.VMEM((W, T), f32)],
        compiler_params=_params(("arbitrary", "arbitrary")),
        name="moba_prompt",
    )(tab, q, k, v)


def _layer_out_kernel(x_ref, ya_ref, ym_ref, yc_ref, sg_ref, g1_ref, sh2_ref, sc2_ref, g2_ref,
                      wa_ref, wm_ref, wc_ref, wo_ref, l1g_ref, l1b_ref, l2g_ref, l2b_ref,
                      wg_ref, wu_ref, wd_ref, o_ref, x1_ref, h2_ref, f_ref, *, alpha):
    kf = pl.program_id(2)
    d = D_MODEL

    @pl.when(kf == 0)
    def _():
        pa = _dot(ya_ref[0], wa_ref[...])
        pm = _dot(ym_ref[0], wm_ref[...])
        pc = _dot(yc_ref[0], wc_ref[...])
        pre = (sg_ref[0, :, :d].astype(f32) * pa + sg_ref[0, :, d:2 * d].astype(f32) * pm
               + sg_ref[0, :, 2 * d:].astype(f32) * pc)
        mix = _dot(pre.astype(wo_ref.dtype), wo_ref[...])
        x1 = _layer_norm(alpha * x_ref[0] + g1_ref[0] * mix, l1g_ref[...], l1b_ref[...])
        x1_ref[...] = x1
        h2_ref[...] = (x1 * (1.0 + sc2_ref[0]) + sh2_ref[0]).astype(h2_ref.dtype)
        f_ref[...] = jnp.zeros_like(f_ref)

    h2 = h2_ref[...]
    t = _silu(_dot(h2, wg_ref[...])) * _dot(h2, wu_ref[...])
    f_ref[...] += _dot(t.astype(wd_ref.dtype), wd_ref[...])

    @pl.when(kf == pl.num_programs(2) - 1)
    def _():
        o_ref[0] = _layer_norm(alpha * x1_ref[...] + g2_ref[0] * f_ref[...], l2g_ref[...], l2b_ref[...])


def layer_out(x, ya, ym, yc, sg, mods, wa, wm, wc, wo, l1g, l1b, l2g, l2b, wg, wu, wd, tm, tf, alpha):
    B, S, d = x.shape
    dff = wg.shape[1]
    tok = lambda a: pl.BlockSpec((1, tm, a.shape[2]), lambda b, i, k: (b, i, 0))
    full = lambda a: pl.BlockSpec(a.shape, lambda b, i, k: (0,) * a.ndim, pipeline_mode=pl.Buffered(1))
    return pl.pallas_call(
        functools.partial(_layer_out_kernel, alpha=alpha),
        grid=(B, S // tm, dff // tf),
        in_specs=[tok(x), tok(ya), tok(ym), tok(yc), tok(sg),
                  _mod_spec(mods, tm, 2), _mod_spec(mods, tm, 3), _mod_spec(mods, tm, 4), _mod_spec(mods, tm, 5),
                  full(wa), full(wm), full(wc), full(wo), full(l1g), full(l1b), full(l2g), full(l2b),
                  pl.BlockSpec((d, tf), lambda b, i, k: (0, k)),
                  pl.BlockSpec((d, tf), lambda b, i, k: (0, k)),
                  pl.BlockSpec((tf, d), lambda b, i, k: (k, 0))],
        out_specs=pl.BlockSpec((1, tm, d), lambda b, i, k: (b, i, 0)),
        out_shape=jax.ShapeDtypeStruct((B, S, d), f32),
        scratch_shapes=[pltpu.VMEM((tm, d), f32), pltpu.VMEM((tm, d), bf16), pltpu.VMEM((tm, d), f32)],
        compiler_params=_params(("arbitrary", "arbitrary", "arbitrary")),
        name="layer_out",
    )(x, ya, ym, yc, sg, mods, mods, mods, mods, wa, wm, wc, wo, l1g, l1b, l2g, l2b, wg, wu, wd)


def _absorb_kernel(qh_ref, wabs_ref, o_ref):
    for h in range(A_HEADS):
        blk = slice(h * HEAD_BLK, (h + 1) * HEAD_BLK)
        o_ref[:, blk] = _dot(qh_ref[:, blk], wabs_ref[h], HI)


def mla_absorb(qh, wabs):
    return pl.pallas_call(_absorb_kernel, out_shape=jax.ShapeDtypeStruct(qh.shape, f32),
                          compiler_params=_params(None), name="mla_absorb")(qh, wabs)


def _paged_kernel(pt_ref, qa_ref, qb_ref, latn_ref, krn_ref, qcol_ref, lat_hbm, krt_hbm, kt_hbm,
                  olat_ref, idx_ref, lat_buf, kr_buf, kt_buf, sems, m_s, l_s, acc_s, g_s,
                  *, layer, pages, cps):
    g = pl.program_id(0)
    c = g % cps
    slot = g % 2
    ppb = C_BLOCK // PAGE_SIZE
    blocks = pages // ppb

    def copies(step, sl):
        b = step // cps
        first = (step % cps) * pages
        out = []
        for j in range(pages):
            pg = pt_ref[b, first + j]
            out.append(pltpu.make_async_copy(lat_hbm.at[layer, pg], lat_buf.at[sl, j], sems.at[sl, 0]))
            out.append(pltpu.make_async_copy(krt_hbm.at[layer, pg], kr_buf.at[sl, j], sems.at[sl, 1]))
            out.append(pltpu.make_async_copy(kt_hbm.at[layer, pg], kt_buf.at[sl, j], sems.at[sl, 2]))
        return out

    @pl.when(g == 0)
    def _():
        for cp in copies(0, 0):
            cp.start()

    @pl.when(g + 1 < pl.num_programs(0))
    def _():
        for cp in copies(g + 1, 1 - slot):
            cp.start()

    qa = qa_ref[0]
    qr = qb_ref[0][:, A_NOPE:A_NOPE + A_ROPE]

    @pl.when(c == 0)
    def _():
        latn = latn_ref[0]
        s0 = jnp.sum(qa * latn, -1, keepdims=True) + jnp.sum(qr * krn_ref[0], -1, keepdims=True)
        m_s[...] = s0
        l_s[...] = jnp.ones_like(l_s)
        acc_s[...] = jnp.broadcast_to(latn, acc_s.shape)

    for cp in copies(g, slot):
        cp.wait()

    lat = lat_buf[slot].reshape(pages * PAGE_SIZE, A_KV_LORA).astype(bf16)
    krt = jnp.concatenate([kr_buf[slot, j] for j in range(pages)], axis=1).astype(bf16)
    s = _nt(qa.astype(bf16), lat) + _dot(qr.astype(bf16), krt)
    m_old = m_s[...]
    m_new = jnp.maximum(m_old, jnp.max(s, -1, keepdims=True))
    alpha = jnp.exp(m_old - m_new)
    p = jnp.exp(s - m_new)
    l_s[...] = alpha * l_s[...] + jnp.sum(p, -1, keepdims=True)
    acc_s[...] = alpha * acc_s[...] + _dot(p.astype(bf16), lat)
    m_s[...] = m_new

    qcb = jnp.broadcast_to(qcol_ref[0], (C_WIDTH, PAGE_SIZE))
    for jj in range(blocks):
        ks = kt_buf[slot, jj * ppb]
        for t in range(1, ppb):
            ks = ks + kt_buf[slot, jj * ppb + t]
        w = ks * qcb
        for h in range(C_HEADS):
            row = jnp.sum(w[h * C_DH:(h + 1) * C_DH, :], axis=0, keepdims=True)
            g_s[pl.ds(h * (cps * blocks) + c * blocks + jj, 1), :] = row

    @pl.when(c == cps - 1)
    def _():
        olat_ref[0] = acc_s[...] / l_s[...]
        nbp = cps * blocks
        gsum = _nt(jnp.ones((SUBLANES, PAGE_SIZE), f32), g_s[...], HI) * (1.0 / C_BLOCK)
        row = lax.broadcasted_iota(jnp.int32, gsum.shape, 0)
        lane_i = lax.broadcasted_iota(jnp.int32, gsum.shape, 1)
        lane = lane_i.astype(f32)
        gm = jnp.where(lane_i // nbp == row, gsum, -jnp.inf)
        out_lane = lax.broadcasted_iota(jnp.int32, (SUBLANES, LANES), 1)
        out = jnp.zeros((SUBLANES, LANES), f32)
        for r in range(C_TOPK):
            mx = jnp.max(gm, -1, keepdims=True)
            ix = jnp.min(jnp.where(gm == mx, lane, float(gm.shape[1])), -1, keepdims=True)
            out = jnp.where(out_lane == r, ix, out)
            gm = jnp.where(lane == ix, -jnp.inf, gm)
        head_off = (lax.broadcasted_iota(jnp.int32, (SUBLANES, LANES), 0) * nbp).astype(f32)
        idx_ref[0] = (out - head_off).astype(jnp.int32)


def sample_paged(page_table, qa, qb, latn, krn, qcol, cache_lat, cache_krt, cache_kt, layer, pages):
    DB, n_pages = page_table.shape
    cps = n_pages // pages
    nbp = n_pages * PAGE_SIZE // C_BLOCK
    seq = lambda a: pl.BlockSpec((1,) + a.shape[1:], lambda g, pt: (g // cps, 0, 0))
    hbm = pl.BlockSpec(memory_space=pl.ANY)
    grid_spec = pltpu.PrefetchScalarGridSpec(
        num_scalar_prefetch=1,
        grid=(DB * cps,),
        in_specs=[seq(qa), seq(qb), seq(latn), seq(krn), seq(qcol), hbm, hbm, hbm],
        out_specs=[pl.BlockSpec((1, A_HEADS, A_KV_LORA), lambda g, pt: (g // cps, 0, 0)),
                   pl.BlockSpec((1, SUBLANES, LANES), lambda g, pt: (g // cps, 0, 0))],
        scratch_shapes=[pltpu.VMEM((2, pages, PAGE_SIZE, A_KV_LORA), f32),
                        pltpu.VMEM((2, pages, A_ROPE, PAGE_SIZE), f32),
                        pltpu.VMEM((2, pages, C_WIDTH, PAGE_SIZE), f32),
                        pltpu.SemaphoreType.DMA((2, 3)),
                        pltpu.VMEM((A_HEADS, 1), f32), pltpu.VMEM((A_HEADS, 1), f32),
                        pltpu.VMEM((A_HEADS, A_KV_LORA), f32),
                        pltpu.VMEM((C_HEADS * nbp, PAGE_SIZE), f32)],
    )
    return pl.pallas_call(
        functools.partial(_paged_kernel, layer=layer, pages=pages, cps=cps),
        grid_spec=grid_spec,
        out_shape=[jax.ShapeDtypeStruct((DB, A_HEADS, A_KV_LORA), f32),
                   jax.ShapeDtypeStruct((DB, SUBLANES, LANES), jnp.int32)],
        compiler_params=_params(("arbitrary",)),
        name="sample_paged",
    )(page_table, qa, qb, latn, krn, qcol, cache_lat, cache_krt, cache_kt)


def _value_up_kernel(o_ref, w_ref, y_ref):
    y_ref[...] = _dot(o_ref[...], w_ref[...], HI).astype(y_ref.dtype)


def mla_value_up(olat, wuv_bd):
    return pl.pallas_call(_value_up_kernel,
                          out_shape=jax.ShapeDtypeStruct((olat.shape[0], wuv_bd.shape[1]), bf16),
                          compiler_params=_params(None), name="mla_value_up")(olat, wuv_bd)


def _mlstm_step_kernel(zm_ref, cs_ref, cw_ref, cb_ref, wqt_ref, wkt_ref, wvt_ref, bi_ref, bf_ref,
                       c0_ref, n0_ref, m0_ref, h_ref, c1_ref, n1_ref, m1_ref, qt_s, kt_s):
    W = M_WIDTH
    zm = zm_ref[...]
    u = zm[:, :W]
    acc = u * cw_ref[M_CONV - 1:M_CONV, :]
    for j in range(M_CONV - 1):
        acc = acc + cs_ref[j] * cw_ref[j:j + 1, :]
    uc = _silu(acc + cb_ref[...])
    qt_s[...] = _nt(wqt_ref[...], uc, HI)
    kt_s[...] = _nt(wkt_ref[...], uc, HI)
    vt = _nt(wvt_ref[...], u, HI)
    gate_t = jax.nn.sigmoid(zm[:, W:2 * W]).T
    i_t = (zm[:, 2 * W:2 * W + LANES] + bi_ref[...]).T[:SUBLANES, :]
    f_t = _log_sigmoid(zm[:, 2 * W + LANES:] + bf_ref[...]).T[:SUBLANES, :]
    outs = []
    for h in range(M_HEADS):
        rows = slice(h * M_DH, (h + 1) * M_DH)
        i_h = i_t[h:h + 1, :]
        a = f_t[h:h + 1, :] + m0_ref[h:h + 1, :]
        mt = jnp.maximum(a, i_h)
        w_old = jnp.exp(a - mt)
        w_tok = jnp.exp(i_h - mt)
        qh = qt_s[rows, :]
        kh = kt_s[rows, :]
        vh = vt[rows, :]
        s = jnp.sum(qh * kh, axis=0, keepdims=True) * w_tok
        qn = jnp.sum(qh * n0_ref[h], axis=0, keepdims=True)

        def body(d, qc, h=h, w_old=w_old, w_tok=w_tok, vh=vh):
            c0 = c0_ref[h, d]
            k_row = kt_s[pl.ds(h * M_DH + d, 1), :]
            q_row = qt_s[pl.ds(h * M_DH + d, 1), :]
            c1_ref[h, d] = w_old * c0 + (w_tok * k_row) * vh
            return qc + q_row * c0

        qc = lax.fori_loop(0, M_DH, body, jnp.zeros((M_DH, zm.shape[0]), f32))
        num = w_old * qc + s * vh
        den = w_old * qn + s
        outs.append(num / jnp.maximum(jnp.abs(den), jnp.exp(-mt)) * gate_t[rows, :])
        n1_ref[h] = w_old * n0_ref[h] + w_tok * kh
        m1_ref[h:h + 1, :] = mt
    h_ref[...] = jnp.concatenate(outs, axis=0).T.astype(h_ref.dtype)


def mlstm_sample_step(zm, conv_t, cw, cb, wqt, wkt, wvt, bi, bfg, c0, n0, m0):
    DB = zm.shape[0]
    shapes = [jax.ShapeDtypeStruct((DB, M_WIDTH), bf16), jax.ShapeDtypeStruct(c0.shape, f32),
              jax.ShapeDtypeStruct(n0.shape, f32), jax.ShapeDtypeStruct(m0.shape, f32)]
    return pl.pallas_call(
        _mlstm_step_kernel, out_shape=shapes,
        scratch_shapes=[pltpu.VMEM((M_WIDTH, DB), f32), pltpu.VMEM((M_WIDTH, DB), f32)],
        compiler_params=_params(None), name="mlstm_sample_step",
    )(zm, conv_t, cw, cb, wqt, wkt, wvt, bi, bfg, c0, n0, m0)


def _moba_decode_kernel(pt_ref, idx_ref, tab_ref, q_ref, kn_ref, vn_ref, kt_hbm, vt_hbm, o_ref,
                        k_buf, v_buf, sems, *, layer, n_sel, past):
    b = pl.program_id(0)
    slot = b % 2
    ppb = C_BLOCK // PAGE_SIZE
    scale = C_DH ** -0.5

    def copies(seq, sl):
        out = []
        for h in range(C_HEADS):
            for j in range(n_sel):
                blk = idx_ref[(seq * C_HEADS + h) * C_TOPK + j // ppb]
                pg = pt_ref[seq, blk * ppb + j % ppb]
                rows = pl.ds(h * C_DH, C_DH)
                out.append(pltpu.make_async_copy(kt_hbm.at[layer, pg, rows], k_buf.at[sl, h * n_sel + j], sems.at[sl, 0]))
                out.append(pltpu.make_async_copy(vt_hbm.at[layer, pg, rows], v_buf.at[sl, h * n_sel + j], sems.at[sl, 1]))
        return out

    @pl.when(b == 0)
    def _():
        for cp in copies(0, 0):
            cp.start()

    @pl.when(b + 1 < pl.num_programs(0))
    def _():
        for cp in copies(b + 1, 1 - slot):
            cp.start()

    for cp in copies(b, slot):
        cp.wait()

    t = lax.broadcasted_iota(jnp.int32, (1, C_BLOCK), 1)
    for h in range(C_HEADS):
        q = q_ref[h]
        kt = jnp.concatenate([k_buf[slot, h * n_sel + j] for j in range(n_sel)], axis=1).astype(bf16)
        vt = jnp.concatenate([v_buf[slot, h * n_sel + j] for j in range(n_sel)], axis=1).astype(bf16)
        q8 = jnp.broadcast_to((q * scale).astype(bf16), (SUBLANES, C_DH))
        s = _dot(q8, kt)[:1, :]
        bias = [_rel_bias(past - (idx_ref[(b * C_HEADS + h) * C_TOPK + j] * C_BLOCK + t), tab_ref, h)
                for j in range(n_sel // ppb)]
        s = s + jnp.concatenate(bias, axis=1)
        s_self = jnp.sum(q * kn_ref[h], -1, keepdims=True) * scale + tab_ref[h * REL_BUCKETS]
        m = jnp.maximum(jnp.max(s, -1, keepdims=True), s_self)
        p = jnp.exp(s - m)
        p_self = jnp.exp(s_self - m)
        l = jnp.sum(p, -1, keepdims=True) + p_self
        pv = _nt(jnp.broadcast_to(p.astype(bf16), (SUBLANES, p.shape[1])), vt)[:1, :]
        o_ref[h] = (pv + p_self * vn_ref[h]) / l


def moba_decode(page_table, idx_flat, tab, q, kn, vn, cache_kt, cache_vt, layer):
    DB, n_pages = page_table.shape
    ppb = C_BLOCK // PAGE_SIZE
    n_sel = C_TOPK * ppb
    past = n_pages * PAGE_SIZE
    seq = pl.BlockSpec((None, C_HEADS, 1, C_DH), lambda b, pt, idx: (b, 0, 0, 0))
    hbm = pl.BlockSpec(memory_space=pl.ANY)
    grid_spec = pltpu.PrefetchScalarGridSpec(
        num_scalar_prefetch=2,
        grid=(DB,),
        in_specs=[pl.BlockSpec(memory_space=pltpu.SMEM), seq, seq, seq, hbm, hbm],
        out_specs=seq,
        scratch_shapes=[pltpu.VMEM((2, C_HEADS * n_sel, C_DH, PAGE_SIZE), f32),
                        pltpu.VMEM((2, C_HEADS * n_sel, C_DH, PAGE_SIZE), f32),
                        pltpu.SemaphoreType.DMA((2, 2))],
    )
    return pl.pallas_call(
        functools.partial(_moba_decode_kernel, layer=layer, n_sel=n_sel, past=past),
        grid_spec=grid_spec,
        out_shape=jax.ShapeDtypeStruct((DB, C_HEADS, 1, C_DH), f32),
        compiler_params=_params(("arbitrary",)),
        name="moba_decode",
    )(page_table, idx_flat, tab, q, kn, vn, cache_kt, cache_vt)


def _pack_w_in(w):
    d = w.shape[0]
    z = lambda n: jnp.zeros((d, n), w.dtype)
    o = 0
    parts = {}
    for name, size in (("cq", A_Q_LORA), ("ckv", A_KV_LORA), ("kr", A_ROPE), ("u", M_WIDTH), ("o", M_WIDTH),
                       ("i", M_HEADS), ("f", M_HEADS), ("qkv", 3 * C_WIDTH), ("g", 3 * D_MODEL)):
        parts[name] = w[:, o:o + size]
        o += size
    kr = parts["kr"]
    half = A_ROPE // 2
    pad = LANES - A_NOPE - A_ROPE
    krp = jnp.concatenate([z(A_NOPE), kr, z(pad)], axis=1)
    krs = jnp.concatenate([z(A_NOPE), kr[:, half:], kr[:, :half], z(pad)], axis=1)
    return jnp.concatenate([parts["cq"], parts["ckv"], krp, krs, parts["u"], parts["o"],
                            parts["i"], z(LANES - M_HEADS), parts["f"], z(LANES - M_HEADS),
                            parts["qkv"], parts["g"]], axis=1)


def _pack_w_q(w):
    r = w.shape[0]
    w3 = w.reshape(r, A_HEADS, A_NOPE + A_ROPE)
    half = A_ROPE // 2
    pad = jnp.zeros((r, A_HEADS, LANES - A_NOPE - A_ROPE), w.dtype)
    nope, rope = w3[..., :A_NOPE], w3[..., A_NOPE:]
    w1 = jnp.concatenate([nope, rope, pad], axis=-1)
    w2 = jnp.concatenate([jnp.zeros_like(nope), rope[..., half:], rope[..., :half], pad], axis=-1)
    return w1.reshape(r, -1), w2.reshape(r, -1)


def _rope_tables(pos):
    inv = ROPE_THETA ** (-jnp.arange(0, A_ROPE, 2, dtype=f32) / A_ROPE)
    ang = pos.astype(f32)[:, None] * inv[None, :]
    cos, sin = jnp.cos(ang), jnp.sin(ang)
    n = pos.shape[0]
    pad = jnp.zeros((n, LANES - A_NOPE - A_ROPE), f32)
    ct = jnp.concatenate([jnp.ones((n, A_NOPE), f32), cos, cos, pad], axis=1)
    st = jnp.concatenate([jnp.zeros((n, A_NOPE), f32), -sin, sin, pad], axis=1)
    return ct, st


def _block_diag(w):
    hN, a, b = w.shape
    eye = jnp.eye(hN, dtype=w.dtype)
    return (eye[:, None, :, None] * w[:, :, None, :]).reshape(hN * a, hN * b)


def _lane_pad(v, width=LANES):
    return jnp.pad(v, ((0, 0), (0, width - v.shape[-1])))


def kernel(x_prompt, x_sample, cache_mla_latent, cache_mla_krope, cache_moba_k, cache_moba_v, state_mlstm_C, state_mlstm_n, state_mlstm_m, state_mlstm_conv, page_table, c_prompt, c_sample, rel_table, w_ada, b_ada, w_in, g_q_norm, w_q_up, g_kv_norm, w_kv_up, conv_w, conv_b, w_mq, w_mk, w_mv, b_i, b_f, w_br_a, w_br_m, w_br_c, w_out, ln1_g, ln1_b, w_ff_gate, w_ff_up, w_ff_down, ln2_g, ln2_b):
    B, S, d = x_prompt.shape
    DB, T, _ = x_sample.shape
    depth = w_in.shape[0]
    n_pool = cache_mla_latent.shape[1]
    n_pages = page_table.shape[1]
    past = n_pages * PAGE_SIZE
    assert T == 1 and past % C_BLOCK == 0 and past // C_BLOCK >= C_TOPK and S % C_BLOCK == 0
    alpha = (2 * depth) ** 0.25
    dff = w_ff_gate.shape[2]

    mods_all = ada_mods(jnp.concatenate([c_prompt, c_sample], axis=0), w_ada, b_ada)
    ct_p, st_p = _rope_tables(jnp.arange(S, dtype=jnp.int32))
    ct_s, st_s = _rope_tables(jnp.full((DB,), past, jnp.int32))
    tab = rel_table.T.reshape(-1)
    cache_kt = jnp.transpose(cache_moba_k, (0, 1, 3, 4, 2)).reshape(depth, n_pool, C_WIDTH, PAGE_SIZE)
    cache_vt = jnp.transpose(cache_moba_v, (0, 1, 3, 4, 2)).reshape(depth, n_pool, C_WIDTH, PAGE_SIZE)
    cache_krt = jnp.transpose(cache_mla_krope, (0, 1, 3, 2))
    xs = x_sample.reshape(1, DB, d)
    xp = x_prompt

    new_p = [[] for _ in range(8)]
    new_s = [[] for _ in range(8)]
    for l in range(depth):
        w_pack = _pack_w_in(w_in[l]).astype(bf16)
        wq1, wq2 = _pack_w_q(w_q_up[l])
        w_uk = w_kv_up[l][..., :A_NOPE]
        w_uv = w_kv_up[l][..., A_NOPE:]
        wuk = jnp.concatenate([w_uk, jnp.zeros_like(w_uk)], axis=-1).reshape(A_KV_LORA, -1)
        wuv = w_uv.reshape(A_KV_LORA, -1)
        gq = g_q_norm[l][None, :]
        gkv = g_kv_norm[l][None, :]
        wq_bd = _block_diag(w_mq[l])
        wk_bd = _block_diag(w_mk[l]) * (M_DH ** -0.5)
        wv_bd = _block_diag(w_mv[l])
        bi = _lane_pad(b_i[l][None, :])
        bfg = _lane_pad(b_f[l][None, :])
        cw = conv_w[l]
        cb = conv_b[l][None, :]
        tail_w = (w_br_a[l].astype(bf16), w_br_m[l].astype(bf16), w_br_c[l].astype(bf16), w_out[l].astype(bf16),
                  ln1_g[l][None, :], ln1_b[l][None, :], ln2_g[l][None, :], ln2_b[l][None, :],
                  w_ff_gate[l].astype(bf16), w_ff_up[l].astype(bf16), w_ff_down[l].astype(bf16))

        mods_p = mods_all[l, :B].reshape(B, 1, 6 * d)
        za, zm, qc, kc, vc, sg = in_proj(xp, mods_p, w_pack, tm=512)
        qh, kh, vh, lat, krope = mla_prep(za, ct_p, st_p, gq, gkv, wq1.astype(bf16), wq2.astype(bf16),
                                          wuk.astype(bf16), wuv.astype(bf16), tm=512, qk_dtype=bf16)
        ya = mla_attn(qh, kh, vh, tq=512)
        ym, c_st, n_st, m_st = mlstm_prompt(zm, cw, cb, wq_bd.astype(bf16), wk_bd.astype(bf16),
                                            wk_bd.T.astype(bf16), wv_bd.astype(bf16), bi, bfg, L=256)
        yc = moba_prompt(qc, kc, vc, tab)
        xp_new = layer_out(xp, ya, ym, yc, sg, mods_p, *tail_w, tm=512, tf=dff // 2, alpha=alpha)
        C1 = jnp.stack([c_st[:, h * M_DH:(h + 1) * M_DH, h * M_DH:(h + 1) * M_DH] for h in range(M_HEADS)], axis=1)
        n1 = jnp.stack([n_st[:, h * M_DH:(h + 1) * M_DH, h] for h in range(M_HEADS)], axis=1)
        m1 = m_st[:, :M_HEADS, 0]
        u_p = zm[:, :, :M_WIDTH]
        conv_p = jnp.pad(u_p, ((0, 0), (M_CONV - 1, 0), (0, 0)))[:, -(M_CONV - 1):]
        for lst, val in zip(new_p, (lat, krope, kc.reshape(B, S, C_HEADS, C_DH), vc.reshape(B, S, C_HEADS, C_DH),
                                    C1, n1, m1, conv_p)):
            lst.append(val)
        xp = xp_new

        mods_s = mods_all[l, B:].reshape(1, DB, 6 * d)
        za, zm, qc, kc, vc, sg = in_proj(xs, mods_s, w_pack, tm=DB)
        qh, _, _, lat, krope = mla_prep(za, ct_s, st_s, gq, gkv, wq1, wq2, wuk, wuv, tm=DB, qk_dtype=f32)
        wabs = jnp.concatenate([jnp.transpose(w_uk, (1, 2, 0)),
                                jnp.zeros((A_HEADS, HEAD_BLK - A_NOPE, A_KV_LORA), f32)], axis=1)
        qh2 = qh.reshape(DB, A_HEADS * HEAD_BLK)
        qabs = mla_absorb(qh2, wabs)
        olat, idx = sample_paged(page_table, qabs.reshape(DB, A_HEADS, A_KV_LORA), qh2.reshape(DB, A_HEADS, HEAD_BLK),
                                 lat.reshape(DB, 1, A_KV_LORA), krope.reshape(DB, 1, A_ROPE),
                                 qc.reshape(DB, C_WIDTH, 1), cache_mla_latent, cache_krt, cache_kt, l, pages=64)
        wuv_bd = _block_diag(jnp.transpose(w_uv, (1, 0, 2)))
        ya = mla_value_up(olat.reshape(DB, A_HEADS * A_KV_LORA), wuv_bd)

        zm2 = zm.reshape(DB, ZM_W)
        conv_state = state_mlstm_conv[l]
        ym, C1t, n1t, m1t = mlstm_sample_step(
            zm2, jnp.transpose(conv_state, (1, 0, 2)), cw, cb, wq_bd.T, wk_bd.T, wv_bd.T, bi, bfg,
            jnp.transpose(state_mlstm_C[l], (1, 2, 3, 0)), jnp.transpose(state_mlstm_n[l], (1, 2, 0)),
            jnp.transpose(state_mlstm_m[l], (1, 0)))
        C1 = jnp.transpose(C1t, (3, 0, 1, 2))
        n1 = jnp.transpose(n1t, (2, 0, 1))
        m1 = jnp.transpose(m1t, (1, 0))
        u_s = zm2[:, :M_WIDTH]
        conv_s = jnp.concatenate([conv_state[:, 1:], u_s[:, None, :]], axis=1)

        idx_flat = idx[:, :C_HEADS, :C_TOPK].reshape(-1)
        per_head = lambda a: a.reshape(DB, C_HEADS, 1, C_DH)
        yc = moba_decode(page_table, idx_flat, tab, per_head(qc), per_head(kc), per_head(vc), cache_kt, cache_vt, l)
        xs_new = layer_out(xs, ya.reshape(1, DB, -1), ym.reshape(1, DB, -1), yc.reshape(1, DB, -1).astype(bf16),
                           sg, mods_s, *tail_w, tm=DB, tf=dff // 2, alpha=alpha)
        for lst, val in zip(new_s, (lat.reshape(DB, 1, A_KV_LORA), krope.reshape(DB, 1, A_ROPE),
                                    kc.reshape(DB, 1, C_HEADS, C_DH), vc.reshape(DB, 1, C_HEADS, C_DH),
                                    C1.reshape(DB, M_HEADS, M_DH, M_DH), n1.reshape(DB, M_HEADS, M_DH),
                                    m1.reshape(DB, M_HEADS), conv_s)):
            lst.append(val)
        xs = xs_new

    outs_p = [jnp.stack(a) for a in new_p]
    outs_s = [jnp.stack(a) for a in new_s]
    return (xp, xs.reshape(DB, T, d), *outs_p, *outs_s)
```

```python
import functools
import math

import jax
import jax.numpy as jnp
from jax import lax
from jax.experimental import pallas as pl
from jax.experimental.pallas import tpu as pltpu

f32 = jnp.float32
bf16 = jnp.bfloat16
HI = lax.Precision.HIGHEST

D_MODEL = 1024
PAGE_SIZE = 128
A_HEADS = 8
A_NOPE = 64
A_ROPE = 32
A_VDIM = 64
A_Q_LORA = 256
A_KV_LORA = 128
ROPE_THETA = 10000.0
M_HEADS = 4
M_DH = 64
M_WIDTH = M_HEADS * M_DH
M_CONV = 4
C_HEADS = 4
C_DH = 64
C_WIDTH = C_HEADS * C_DH
C_BLOCK = 256
C_TOPK = 3
REL_BUCKETS = 32
REL_MAX_DIST = 128
LN_EPS = 1e-5
RMS_EPS = 1e-6
LANES = 128
SUBLANES = 8
VMEM_LIMIT = 56 * 1024 * 1024
NEG = -1e30

ZA_W = A_Q_LORA + A_KV_LORA + 2 * LANES
ZM_W = 2 * M_WIDTH + 2 * LANES
ZC_W = 3 * C_WIDTH
ZG_W = 3 * D_MODEL
PACK_W = ZA_W + ZM_W + ZC_W + ZG_W
HEAD_BLK = LANES


def _rel_thresholds():
    exact = REL_BUCKETS // 2
    out = []
    for k in range(1, REL_BUCKETS - exact):
        n = exact
        while int(math.log(n / exact) / math.log(REL_MAX_DIST / exact) * (REL_BUCKETS - exact)) < k:
            n += 1
        out.append(n)
    return tuple(out)


REL_THRESH = _rel_thresholds()


def _nt(a, b, precision=None):
    return lax.dot_general(a, b, (((1,), (1,)), ((), ())), precision=precision, preferred_element_type=f32)


def _tn(a, b):
    return lax.dot_general(a, b, (((0,), (0,)), ((), ())), preferred_element_type=f32)


def _dot(a, b, precision=None):
    return jnp.dot(a, b, precision=precision, preferred_element_type=f32)


def _silu(x):
    return x * jax.nn.sigmoid(x)


def _layer_norm(x, g, b):
    mu = jnp.mean(x, -1, keepdims=True)
    xc = x - mu
    var = jnp.mean(xc * xc, -1, keepdims=True)
    return xc * lax.rsqrt(var + LN_EPS) * g + b


def _rms_norm(x, g):
    return x * lax.rsqrt(jnp.mean(x * x, -1, keepdims=True) + RMS_EPS) * g


def _rel_bias(dist, tab_ref, h):
    n = jnp.maximum(dist, 0)
    exact = REL_BUCKETS // 2
    large = jnp.full(n.shape, exact, jnp.int32)
    for t in REL_THRESH:
        large = large + (n >= t).astype(jnp.int32)
    bucket = jnp.where(n < exact, n, large)
    val = jnp.full(n.shape, tab_ref[h * REL_BUCKETS + REL_BUCKETS - 1], f32)
    for b in range(REL_BUCKETS - 1):
        val = jnp.where(bucket == b, tab_ref[h * REL_BUCKETS + b], val)
    return val


def _params(sem):
    return pltpu.CompilerParams(dimension_semantics=sem, vmem_limit_bytes=VMEM_LIMIT)


def _ada_kernel(c_ref, w_ref, b_ref, o_ref):
    c = c_ref[...]
    o_ref[...] = _dot(_silu(c), w_ref[...], HI) + b_ref[...]


def ada_mods(c_all, w_ada, b_ada):
    depth, d, n = w_ada.shape
    rows = c_all.shape[0]
    tn = n // 4
    return pl.pallas_call(
        _ada_kernel,
        grid=(depth, n // tn),
        in_specs=[
            pl.BlockSpec((rows, d), lambda l, j: (0, 0)),
            pl.BlockSpec((None, d, tn), lambda l, j: (l, 0, j)),
            pl.BlockSpec((None, 1, tn), lambda l, j: (l, 0, j)),
        ],
        out_specs=pl.BlockSpec((None, rows, tn), lambda l, j: (l, 0, j)),
        out_shape=jax.ShapeDtypeStruct((depth, rows, n), f32),
        compiler_params=_params(("arbitrary", "arbitrary")),
        name="ada_mods",
    )(c_all, w_ada, b_ada.reshape(depth, 1, n))


def _mod_spec(mods, tm, col):
    per_row = mods.shape[1] != 1
    rows = tm if per_row else 1
    return pl.BlockSpec((1, rows, D_MODEL), lambda b, i, *_: (b, i if per_row else 0, col))


def _in_proj_kernel(x_ref, sh_ref, sc_ref, w_ref, za_ref, zm_ref, q_ref, k_ref, v_ref, sg_ref):
    h = (x_ref[0] * (1.0 + sc_ref[0]) + sh_ref[0]).astype(w_ref.dtype)

    def mm(c0, c1):
        return _dot(h, w_ref[:, c0:c1])

    c = 0
    za_ref[0] = mm(c, c + ZA_W)
    c += ZA_W
    zm_ref[0] = mm(c, c + ZM_W)
    c += ZM_W
    q_ref[0] = mm(c, c + C_WIDTH)
    k_ref[0] = mm(c + C_WIDTH, c + 2 * C_WIDTH)
    v_ref[0] = mm(c + 2 * C_WIDTH, c + 3 * C_WIDTH)
    c += ZC_W
    sg_ref[0] = jax.nn.sigmoid(mm(c, c + ZG_W)).astype(sg_ref.dtype)


def in_proj(x, mods, w_pack, tm):
    B, S, d = x.shape
    widths = (ZA_W, ZM_W, C_WIDTH, C_WIDTH, C_WIDTH, ZG_W)
    dts = (f32, f32, f32, f32, f32, bf16)
    return pl.pallas_call(
        _in_proj_kernel,
        grid=(B, S // tm),
        in_specs=[
            pl.BlockSpec((1, tm, d), lambda b, i: (b, i, 0)),
            _mod_spec(mods, tm, 0),
            _mod_spec(mods, tm, 1),
            pl.BlockSpec((d, PACK_W), lambda b, i: (0, 0)),
        ],
        out_specs=[pl.BlockSpec((1, tm, w), lambda b, i: (b, i, 0)) for w in widths],
        out_shape=[jax.ShapeDtypeStruct((B, S, w), t) for w, t in zip(widths, dts)],
        compiler_params=_params(("arbitrary", "arbitrary")),
        name="in_proj",
    )(x, mods, mods, w_pack)


def _mla_prep_kernel(za_ref, ct_ref, st_ref, gq_ref, gkv_ref, wq1_ref, wq2_ref, wuk_ref, wuv_ref,
                     qh_ref, kh_ref, vh_ref, lat_ref, kr_ref, *, prec):
    za = za_ref[0]
    cq = za[:, :A_Q_LORA]
    ckv = za[:, A_Q_LORA:A_Q_LORA + A_KV_LORA]
    krp = za[:, A_Q_LORA + A_KV_LORA:A_Q_LORA + A_KV_LORA + LANES]
    krs = za[:, A_Q_LORA + A_KV_LORA + LANES:]
    ct = ct_ref[...]
    st = st_ref[...]
    scale = (A_NOPE + A_ROPE) ** -0.5
    cqn = _rms_norm(cq, gq_ref[...]).astype(wq1_ref.dtype)
    q1 = _dot(cqn, wq1_ref[...], prec)
    q2 = _dot(cqn, wq2_ref[...], prec)
    lat = _rms_norm(ckv, gkv_ref[...])
    lat_ref[0] = lat
    kro = krp * ct + krs * st
    kr_ref[0] = kro[:, A_NOPE:A_NOPE + A_ROPE]
    latc = lat.astype(wuk_ref.dtype)
    kn = _dot(latc, wuk_ref[...], prec)
    for h in range(A_HEADS):
        blk = slice(h * HEAD_BLK, (h + 1) * HEAD_BLK)
        qh_ref[0, :, blk] = ((q1[:, blk] * ct + q2[:, blk] * st) * scale).astype(qh_ref.dtype)
        kh_ref[0, :, blk] = (kn[:, blk] + kro).astype(kh_ref.dtype)
    vh_ref[0] = _dot(latc, wuv_ref[...], prec).astype(vh_ref.dtype)


def mla_prep(za, ctab, stab, gq, gkv, wq1, wq2, wuk, wuv, tm, qk_dtype):
    B, S, _ = za.shape
    full = lambda a: pl.BlockSpec(a.shape, lambda b, i: (0,) * a.ndim)
    tok = lambda w: pl.BlockSpec((1, tm, w), lambda b, i: (b, i, 0))
    widths = (A_HEADS * HEAD_BLK, A_HEADS * HEAD_BLK, A_HEADS * A_VDIM, A_KV_LORA, A_ROPE)
    dts = (qk_dtype, qk_dtype, qk_dtype, f32, f32)
    prec = HI if qk_dtype == f32 else None
    return pl.pallas_call(
        functools.partial(_mla_prep_kernel, prec=prec),
        grid=(B, S // tm),
        in_specs=[tok(ZA_W),
                  pl.BlockSpec((tm, LANES), lambda b, i: (i, 0)),
                  pl.BlockSpec((tm, LANES), lambda b, i: (i, 0)),
                  full(gq), full(gkv), full(wq1), full(wq2), full(wuk), full(wuv)],
        out_specs=[tok(w) for w in widths],
        out_shape=[jax.ShapeDtypeStruct((B, S, w), t) for w, t in zip(widths, dts)],
        compiler_params=_params(("arbitrary", "arbitrary")),
        name="mla_prep",
    )(za, ctab, stab, gq, gkv, wq1, wq2, wuk, wuv)


def _mla_attn_kernel(q_ref, k_ref, v_ref, o_ref, *, tq):
    S = q_ref.shape[1]
    lane = lax.broadcasted_iota(jnp.int32, (tq, 2 * A_VDIM), 1)
    for t in range(S // tq):
        kv_len = (t + 1) * tq
        row = t * tq + lax.broadcasted_iota(jnp.int32, (tq, kv_len), 0)
        col = lax.broadcasted_iota(jnp.int32, (tq, kv_len), 1)
        causal = col <= row
        vpair = v_ref[0, :kv_len, :]
        outs = []
        for hh in range(2):
            blk = slice(hh * HEAD_BLK, (hh + 1) * HEAD_BLK)
            s = _nt(q_ref[0, t * tq:(t + 1) * tq, blk], k_ref[0, :kv_len, blk])
            s = jnp.where(causal, s, -jnp.inf)
            m = jnp.max(s, -1, keepdims=True)
            p = jnp.exp(s - m)
            l = jnp.sum(p, -1, keepdims=True)
            outs.append(_dot(p.astype(vpair.dtype), vpair) / l)
        o_ref[0, t * tq:(t + 1) * tq, :] = jnp.where(lane < A_VDIM, outs[0], outs[1]).astype(o_ref.dtype)


def mla_attn(qh, kh, vh, tq):
    B, S, _ = qh.shape
    pairs = A_HEADS // 2
    return pl.pallas_call(
        functools.partial(_mla_attn_kernel, tq=tq),
        grid=(B, pairs),
        in_specs=[pl.BlockSpec((1, S, 2 * HEAD_BLK), lambda b, p: (b, 0, p)),
                  pl.BlockSpec((1, S, 2 * HEAD_BLK), lambda b, p: (b, 0, p)),
                  pl.BlockSpec((1, S, 2 * A_VDIM), lambda b, p: (b, 0, p))],
        out_specs=pl.BlockSpec((1, S, 2 * A_VDIM), lambda b, p: (b, 0, p)),
        out_shape=jax.ShapeDtypeStruct((B, S, A_HEADS * A_VDIM), bf16),
        compiler_params=_params(("arbitrary", "arbitrary")),
        name="mla_attn",
    )(qh, kh, vh)


def _log_sigmoid(x):
    return jnp.minimum(x, 0.0) - jnp.log1p(jnp.exp(-jnp.abs(x)))


def _mlstm_kernel(zm_ref, cw_ref, cb_ref, wq_ref, wk_ref, wkt_ref, wv_ref, bi_ref, bf_ref,
                  ym_ref, c_out_ref, n_out_ref, m_out_ref,
                  c_st, n_st, m_row, m_col, tail):
    L = zm_ref.shape[1]
    W = M_WIDTH
    c = pl.program_id(1)

    @pl.when(c == 0)
    def _():
        c_st[...] = jnp.zeros_like(c_st)
        n_st[...] = jnp.zeros_like(n_st)
        m_row[...] = jnp.zeros_like(m_row)
        m_col[...] = jnp.zeros_like(m_col)
        tail[...] = jnp.zeros_like(tail)

    zm = zm_ref[0]
    u = zm[:, :W]
    o_raw = zm[:, W:2 * W]
    ig = zm[:, 2 * W:2 * W + LANES]
    fg = zm[:, 2 * W + LANES:]

    rows = lax.broadcasted_iota(jnp.int32, (L, W), 0)
    tl = tail[...]
    acc = u * cw_ref[M_CONV - 1:M_CONV, :]
    for j in range(1, M_CONV):
        ru = pltpu.roll(u, j, 0)
        rt = jnp.concatenate([pltpu.roll(tl, j, 0), jnp.zeros((L - SUBLANES, W), f32)], axis=0)
        acc = acc + jnp.where(rows < j, rt, ru) * cw_ref[M_CONV - 1 - j:M_CONV - j, :]
    tail[...] = u[L - SUBLANES:, :]
    uc = _silu(acc + cb_ref[...]).astype(bf16)
    ub = u.astype(bf16)

    q = _dot(uc, wq_ref[...])
    k = _dot(uc, wk_ref[...])
    kt = _nt(wkt_ref[...], uc)
    v = _dot(ub, wv_ref[...])
    qb = q.astype(bf16)
    kb = k.astype(bf16)
    vb = v.astype(bf16)

    i_c = ig + bi_ref[...]
    f_c = _log_sigmoid(fg + bf_ref[...])
    t_i = lax.broadcasted_iota(jnp.int32, (L, L), 0)
    s_i = lax.broadcasted_iota(jnp.int32, (L, L), 1)
    causal = s_i <= t_i
    tri = causal.astype(f32)
    b_c = _dot(tri, f_c, HI)
    b_r = b_c.T[:SUBLANES, :]
    i_r = i_c.T[:SUBLANES, :]
    a_c = b_c + m_row[...]

    inter = _dot(qb, c_st[...].astype(bf16))
    qn = _dot(qb, n_st[...].astype(bf16))
    lane_head = lax.broadcasted_iota(jnp.int32, (1, W), 1) // M_DH
    intra = jnp.zeros((L, W), f32)
    w_full = jnp.zeros((L, W), f32)
    r_full = jnp.zeros((L, W), f32)
    for h in range(M_HEADS):
        hm = lane_head == h
        dmat = jnp.where(causal, b_c[:, h:h + 1] - b_r[h:h + 1, :] + i_r[h:h + 1, :], -jnp.inf)
        a_h = a_c[:, h:h + 1]
        mt = jnp.maximum(a_h, jnp.max(dmat, -1, keepdims=True))
        pm = jnp.exp(dmat - mt)
        s = _nt(jnp.where(hm, qb, jnp.zeros_like(qb)), kb) * pm
        w_h = jnp.exp(a_h - mt)
        den = w_h * qn[:, h:h + 1] + jnp.sum(s, -1, keepdims=True)
        r_h = 1.0 / jnp.maximum(jnp.abs(den), jnp.exp(-mt))
        intra = intra + _dot(s.astype(bf16), jnp.where(hm, vb, jnp.zeros_like(vb)))
        w_full = jnp.where(hm, w_h, w_full)
        r_full = jnp.where(hm, r_h, r_full)
    hout = (w_full * inter + intra) * r_full
    ym_ref[0] = (hout * jax.nn.sigmoid(o_raw)).astype(ym_ref.dtype)

    b_end = b_r[:, L - 1:L]
    g_r = b_end - b_r + i_r
    mc = m_col[...][:, :1]
    m_new = jnp.maximum(b_end + mc, jnp.max(g_r, -1, keepdims=True))
    w_old = jnp.exp(b_end + mc - m_new)
    w_tok = jnp.exp(g_r - m_new)
    w_tok_full = jnp.concatenate([jnp.broadcast_to(w_tok[h:h + 1, :], (M_DH, L)) for h in range(M_HEADS)], axis=0)
    w_old_full = jnp.concatenate([jnp.broadcast_to(w_old[h:h + 1, :], (M_DH, 1)) for h in range(M_HEADS)], axis=0)
    ktw = (kt * w_tok_full).astype(bf16)
    row_head = lax.broadcasted_iota(jnp.int32, (W, W), 0) // M_DH
    col_head = lax.broadcasted_iota(jnp.int32, (W, W), 1) // M_DH
    c_st[...] = w_old_full * c_st[...] + jnp.where(row_head == col_head, _dot(ktw, vb), 0.0)
    n_rows = lax.broadcasted_iota(jnp.int32, (W, LANES), 0) // M_DH
    n_cols = lax.broadcasted_iota(jnp.int32, (W, LANES), 1)
    ksum = _dot(ktw, jnp.ones((L, LANES), bf16))
    n_st[...] = w_old_full * n_st[...] + jnp.where(n_rows == n_cols, ksum, 0.0)
    eye = (lax.broadcasted_iota(jnp.int32, (SUBLANES, LANES), 0)
           == lax.broadcasted_iota(jnp.int32, (SUBLANES, LANES), 1)).astype(f32)
    m_row[...] = jnp.sum(eye * m_new, axis=0, keepdims=True)
    m_col[...] = jnp.broadcast_to(m_new, (SUBLANES, LANES))

    @pl.when(c == pl.num_programs(1) - 1)
    def _():
        c_out_ref[0] = c_st[...]
        n_out_ref[0] = n_st[...]
        m_out_ref[0] = m_col[...]


def mlstm_prompt(zm, cw, cb, wq, wk, wkt, wv, bi, bfg, L):
    B, S, _ = zm.shape
    W = M_WIDTH
    full = lambda a: pl.BlockSpec(a.shape, lambda b, c: (0,) * a.ndim)
    return pl.pallas_call(
        _mlstm_kernel,
        grid=(B, S // L),
        in_specs=[pl.BlockSpec((1, L, ZM_W), lambda b, c: (b, c, 0)),
                  full(cw), full(cb), full(wq), full(wk), full(wkt), full(wv), full(bi), full(bfg)],
        out_specs=[pl.BlockSpec((1, L, W), lambda b, c: (b, c, 0)),
                   pl.BlockSpec((1, W, W), lambda b, c: (b, 0, 0)),
                   pl.BlockSpec((1, W, LANES), lambda b, c: (b, 0, 0)),
                   pl.BlockSpec((1, SUBLANES, LANES), lambda b, c: (b, 0, 0))],
        out_shape=[jax.ShapeDtypeStruct((B, S, W), bf16),
                   jax.ShapeDtypeStruct((B, W, W), f32),
                   jax.ShapeDtypeStruct((B, W, LANES), f32),
                   jax.ShapeDtypeStruct((B, SUBLANES, LANES), f32)],
        scratch_shapes=[pltpu.VMEM((W, W), f32), pltpu.VMEM((W, LANES), f32),
                        pltpu.VMEM((1, LANES), f32), pltpu.VMEM((SUBLANES, LANES), f32),
                        pltpu.VMEM((SUBLANES, W), f32)],
        compiler_params=_params(("arbitrary", "arbitrary")),
        name="mlstm_prompt",
    )(zm, cw, cb, wq, wk, wkt, wv, bi, bfg)


MOBA_VH = C_DH + 16


def _max_rows(x):
    r = x.shape[0]
    while r > SUBLANES:
        r //= 2
        x = jnp.maximum(x[:r], x[r:])
    return jnp.max(x, axis=0, keepdims=True)


def _moba_prompt_kernel(tab_ref, q_ref, k_ref, v_ref, o_ref,
                        k16, vt16, kmean, b_own, b_prev, sel_ref, qm_ref, acc_ref, s_ref, p_ref):
    S = k_ref.shape[1]
    nb = S // C_BLOCK
    T = C_BLOCK
    H = C_HEADS
    i = pl.program_id(1)
    key_i = lax.broadcasted_iota(jnp.int32, (T, T), 0)
    qry_i = lax.broadcasted_iota(jnp.int32, (T, T), 1)

    @pl.when(i == 0)
    def _():
        kf = k_ref[0]
        ones = jnp.ones((MOBA_VH - C_DH, T), bf16)
        for j in range(nb):
            k16[j] = kf[j * T:(j + 1) * T, :].astype(bf16)
            vt = v_ref[0, j * T:(j + 1) * T, :].T.astype(bf16)
            vt16[j] = jnp.concatenate([x for h in range(H) for x in (vt[h * C_DH:(h + 1) * C_DH], ones)], axis=0)
        kmean[...] = jnp.sum(kf.reshape(nb, T, C_WIDTH), axis=1) * (1.0 / T)
        for h in range(H):
            b_own[h] = _rel_bias(qry_i - key_i, tab_ref, h)
            b_prev[h] = _rel_bias(qry_i - key_i + T, tab_ref, h)

    qf = q_ref[0]
    lane_head = lax.broadcasted_iota(jnp.int32, (1, C_WIDTH), 1) // C_DH
    jio = lax.broadcasted_iota(jnp.int32, (nb, T), 0)
    valid = jio < i
    scale = C_DH ** -0.5
    for h in range(H):
        hm = lane_head == h
        qm_ref[h] = (jnp.where(hm, qf, 0.0) * scale).astype(bf16)
        gs = _nt(jnp.where(hm, kmean[...], 0.0), qf, HI)
        sel = jnp.zeros((nb, T), f32)
        for j in range(nb):
            vj = gs[j:j + 1, :]
            beats = jnp.where(gs > vj, 1.0, jnp.where((gs == vj) & (jio < j), 1.0, 0.0))
            cnt = jnp.sum(jnp.where(valid, beats, 0.0), axis=0, keepdims=True)
            sel = jnp.where(jio == j, jnp.where(cnt < C_TOPK, 1.0, 0.0), sel)
        sel_ref[h] = (sel - 1.0) * (-NEG)

    rows = lambda h: slice(h * MOBA_VH, (h + 1) * MOBA_VH)

    kblk = k16[i]
    for h in range(H):
        s_ref[h] = _nt(kblk, qm_ref[h])
    ms = []
    for h in range(H):
        s = jnp.where(key_i <= qry_i, s_ref[h] + b_own[h], NEG)
        m0 = _max_rows(s)
        p_ref[h] = jnp.exp(s - m0).astype(bf16)
        ms.append(m0)
    for h in range(H):
        acc_ref[rows(h), :] = _dot(vt16[i, rows(h), :], p_ref[h])

    def past_block(j, bias_of, ms):
        kblk = k16[j]
        for h in range(H):
            s_ref[h] = _nt(kblk, qm_ref[h])
        ms_new, alphas = [], []
        for h in range(H):
            sj = s_ref[h] + (bias_of(h) + sel_ref[h, pl.ds(j, 1), :])
            m_new = jnp.maximum(ms[h], _max_rows(sj))
            alphas.append(jnp.exp(ms[h] - m_new))
            p_ref[h] = jnp.exp(sj - m_new).astype(bf16)
            ms_new.append(m_new)
        for h in range(H):
            acc_ref[rows(h), :] = alphas[h] * acc_ref[rows(h), :] + _dot(vt16[j, rows(h), :], p_ref[h])
        return tuple(ms_new)

    ms = tuple(ms)
    ms = lax.cond(i >= 1, lambda c: past_block(i - 1, lambda h: b_prev[h], c), lambda c: c, ms)
    lax.fori_loop(0, jnp.maximum(i - 1, 0),
                  lambda j, c: past_block(j, lambda h: tab_ref[h * REL_BUCKETS + REL_BUCKETS - 1], c), ms)
    res = jnp.concatenate(
        [acc_ref[h * MOBA_VH:h * MOBA_VH + C_DH, :] / acc_ref[h * MOBA_VH + C_DH:h * MOBA_VH + C_DH + 1, :]
         for h in range(H)], axis=0)
    o_ref[0] = res.T.astype(o_ref.dtype)


def moba_prompt(q, k, v, tab):
    B, S, W = q.shape
    nb = S // C_BLOCK
    T = C_BLOCK
    return pl.pallas_call(
        _moba_prompt_kernel,
        grid=(B, nb),
        in_specs=[pl.BlockSpec(memory_space=pltpu.SMEM),
                  pl.BlockSpec((1, T, W), lambda b, i: (b, i, 0)),
                  pl.BlockSpec((1, S, W), lambda b, i: (b, 0, 0)),
                  pl.BlockSpec((1, S, W), lambda b, i: (b, 0, 0))],
        out_specs=pl.BlockSpec((1, T, W), lambda b, i: (b, i, 0)),
        out_shape=jax.ShapeDtypeStruct((B, S, W), bf16),
        scratch_shapes=[pltpu.VMEM((nb, T, W), bf16), pltpu.VMEM((nb, C_HEADS * MOBA_VH, T), bf16),
                        pltpu.VMEM((nb, W), f32),
                        pltpu.VMEM((C_HEADS, T, T), f32), pltpu.VMEM((C_HEADS, T, T), f32),
                        pltpu.VMEM((C_HEADS, nb, T), f32), pltpu.VMEM((C_HEADS, T, W), bf16),
                        pltpu.VMEM((C_HEADS * MOBA_VH, T), f32),
                        pltpu.VMEM((C_HEADS, T, T), f32), pltpu.VMEM((C_HEADS, T, T), bf16)],
        compiler_params=_params(("arbitrary", "arbitrary")),
        name="moba_prompt",
    )(tab, q, k, v)


def _layer_out_kernel(x_ref, ya_ref, ym_ref, yc_ref, sg_ref, g1_ref, sh2_ref, sc2_ref, g2_ref,
                      wa_ref, wm_ref, wc_ref, wo_ref, l1g_ref, l1b_ref, l2g_ref, l2b_ref,
                      wg_ref, wu_ref, wd_ref, o_ref, x1_ref, h2_ref, f_ref, *, alpha):
    kf = pl.program_id(2)
    d = D_MODEL

    @pl.when(kf == 0)
    def _():
        pa = _dot(ya_ref[0], wa_ref[...])
        pm = _dot(ym_ref[0], wm_ref[...])
        pc = _dot(yc_ref[0], wc_ref[...])
        pre = (sg_ref[0, :, :d].astype(f32) * pa + sg_ref[0, :, d:2 * d].astype(f32) * pm
               + sg_ref[0, :, 2 * d:].astype(f32) * pc)
        mix = _dot(pre.astype(wo_ref.dtype), wo_ref[...])
        x1 = _layer_norm(alpha * x_ref[0] + g1_ref[0] * mix, l1g_ref[...], l1b_ref[...])
        x1_ref[...] = x1
        h2_ref[...] = (x1 * (1.0 + sc2_ref[0]) + sh2_ref[0]).astype(h2_ref.dtype)
        f_ref[...] = jnp.zeros_like(f_ref)

    h2 = h2_ref[...]
    t = _silu(_dot(h2, wg_ref[...])) * _dot(h2, wu_ref[...])
    f_ref[...] += _dot(t.astype(wd_ref.dtype), wd_ref[...])

    @pl.when(kf == pl.num_programs(2) - 1)
    def _():
        o_ref[0] = _layer_norm(alpha * x1_ref[...] + g2_ref[0] * f_ref[...], l2g_ref[...], l2b_ref[...])


def layer_out(x, ya, ym, yc, sg, mods, wa, wm, wc, wo, l1g, l1b, l2g, l2b, wg, wu, wd, tm, tf, alpha):
    B, S, d = x.shape
    dff = wg.shape[1]
    tok = lambda a: pl.BlockSpec((1, tm, a.shape[2]), lambda b, i, k: (b, i, 0))
    full = lambda a: pl.BlockSpec(a.shape, lambda b, i, k: (0,) * a.ndim, pipeline_mode=pl.Buffered(1))
    return pl.pallas_call(
        functools.partial(_layer_out_kernel, alpha=alpha),
        grid=(B, S // tm, dff // tf),
        in_specs=[tok(x), tok(ya), tok(ym), tok(yc), tok(sg),
                  _mod_spec(mods, tm, 2), _mod_spec(mods, tm, 3), _mod_spec(mods, tm, 4), _mod_spec(mods, tm, 5),
                  full(wa), full(wm), full(wc), full(wo), full(l1g), full(l1b), full(l2g), full(l2b),
                  pl.BlockSpec((d, tf), lambda b, i, k: (0, k)),
                  pl.BlockSpec((d, tf), lambda b, i, k: (0, k)),
                  pl.BlockSpec((tf, d), lambda b, i, k: (k, 0))],
        out_specs=pl.BlockSpec((1, tm, d), lambda b, i, k: (b, i, 0)),
        out_shape=jax.ShapeDtypeStruct((B, S, d), f32),
        scratch_shapes=[pltpu.VMEM((tm, d), f32), pltpu.VMEM((tm, d), bf16), pltpu.VMEM((tm, d), f32)],
        compiler_params=_params(("arbitrary", "arbitrary", "arbitrary")),
        name="layer_out",
    )(x, ya, ym, yc, sg, mods, mods, mods, mods, wa, wm, wc, wo, l1g, l1b, l2g, l2b, wg, wu, wd)


def _absorb_kernel(qh_ref, wabs_ref, o_ref):
    for h in range(A_HEADS):
        blk = slice(h * HEAD_BLK, (h + 1) * HEAD_BLK)
        o_ref[:, blk] = _dot(qh_ref[:, blk], wabs_ref[h], HI)


def mla_absorb(qh, wabs):
    return pl.pallas_call(_absorb_kernel, out_shape=jax.ShapeDtypeStruct(qh.shape, f32),
                          compiler_params=_params(None), name="mla_absorb")(qh, wabs)


def _paged_kernel(pt_ref, qa_ref, qb_ref, latn_ref, krn_ref, qcol_ref, lat_hbm, krt_hbm, kt_hbm,
                  olat_ref, idx_ref, lat_buf, kr_buf, kt_buf, sems, m_s, l_s, acc_s, g_s,
                  *, layer, pages, cps):
    g = pl.program_id(0)
    c = g % cps
    slot = g % 2
    ppb = C_BLOCK // PAGE_SIZE
    blocks = pages // ppb

    def copies(step, sl):
        b = step // cps
        first = (step % cps) * pages
        out = []
        for j in range(pages):
            pg = pt_ref[b, first + j]
            out.append(pltpu.make_async_copy(lat_hbm.at[layer, pg], lat_buf.at[sl, j], sems.at[sl, 0]))
            out.append(pltpu.make_async_copy(krt_hbm.at[layer, pg], kr_buf.at[sl, j], sems.at[sl, 1]))
            out.append(pltpu.make_async_copy(kt_hbm.at[layer, pg], kt_buf.at[sl, j], sems.at[sl, 2]))
        return out

    @pl.when(g == 0)
    def _():
        for cp in copies(0, 0):
            cp.start()

    @pl.when(g + 1 < pl.num_programs(0))
    def _():
        for cp in copies(g + 1, 1 - slot):
            cp.start()

    qa = qa_ref[0]
    qr = qb_ref[0][:, A_NOPE:A_NOPE + A_ROPE]

    @pl.when(c == 0)
    def _():
        latn = latn_ref[0]
        s0 = jnp.sum(qa * latn, -1, keepdims=True) + jnp.sum(qr * krn_ref[0], -1, keepdims=True)
        m_s[...] = s0
        l_s[...] = jnp.ones_like(l_s)
        acc_s[...] = jnp.broadcast_to(latn, acc_s.shape)

    for cp in copies(g, slot):
        cp.wait()

    lat = lat_buf[slot].reshape(pages * PAGE_SIZE, A_KV_LORA).astype(bf16)
    krt = jnp.concatenate([kr_buf[slot, j] for j in range(pages)], axis=1).astype(bf16)
    s = _nt(qa.astype(bf16), lat) + _dot(qr.astype(bf16), krt)
    m_old = m_s[...]
    m_new = jnp.maximum(m_old, jnp.max(s, -1, keepdims=True))
    alpha = jnp.exp(m_old - m_new)
    p = jnp.exp(s - m_new)
    l_s[...] = alpha * l_s[...] + jnp.sum(p, -1, keepdims=True)
    acc_s[...] = alpha * acc_s[...] + _dot(p.astype(bf16), lat)
    m_s[...] = m_new

    qcb = jnp.broadcast_to(qcol_ref[0], (C_WIDTH, PAGE_SIZE))
    for jj in range(blocks):
        ks = kt_buf[slot, jj * ppb]
        for t in range(1, ppb):
            ks = ks + kt_buf[slot, jj * ppb + t]
        w = ks * qcb
        for h in range(C_HEADS):
            row = jnp.sum(w[h * C_DH:(h + 1) * C_DH, :], axis=0, keepdims=True)
            g_s[pl.ds(h * (cps * blocks) + c * blocks + jj, 1), :] = row

    @pl.when(c == cps - 1)
    def _():
        olat_ref[0] = acc_s[...] / l_s[...]
        nbp = cps * blocks
        gsum = _nt(jnp.ones((SUBLANES, PAGE_SIZE), f32), g_s[...], HI) * (1.0 / C_BLOCK)
        row = lax.broadcasted_iota(jnp.int32, gsum.shape, 0)
        lane_i = lax.broadcasted_iota(jnp.int32, gsum.shape, 1)
        lane = lane_i.astype(f32)
        gm = jnp.where(lane_i // nbp == row, gsum, -jnp.inf)
        out_lane = lax.broadcasted_iota(jnp.int32, (SUBLANES, LANES), 1)
        out = jnp.zeros((SUBLANES, LANES), f32)
        for r in range(C_TOPK):
            mx = jnp.max(gm, -1, keepdims=True)
            ix = jnp.min(jnp.where(gm == mx, lane, float(gm.shape[1])), -1, keepdims=True)
            out = jnp.where(out_lane == r, ix, out)
            gm = jnp.where(lane == ix, -jnp.inf, gm)
        head_off = (lax.broadcasted_iota(jnp.int32, (SUBLANES, LANES), 0) * nbp).astype(f32)
        idx_ref[0] = (out - head_off).astype(jnp.int32)


def sample_paged(page_table, qa, qb, latn, krn, qcol, cache_lat, cache_krt, cache_kt, layer, pages):
    DB, n_pages = page_table.shape
    cps = n_pages // pages
    nbp = n_pages * PAGE_SIZE // C_BLOCK
    seq = lambda a: pl.BlockSpec((1,) + a.shape[1:], lambda g, pt: (g // cps, 0, 0))
    hbm = pl.BlockSpec(memory_space=pl.ANY)
    grid_spec = pltpu.PrefetchScalarGridSpec(
        num_scalar_prefetch=1,
        grid=(DB * cps,),
        in_specs=[seq(qa), seq(qb), seq(latn), seq(krn), seq(qcol), hbm, hbm, hbm],
        out_specs=[pl.BlockSpec((1, A_HEADS, A_KV_LORA), lambda g, pt: (g // cps, 0, 0)),
                   pl.BlockSpec((1, SUBLANES, LANES), lambda g, pt: (g // cps, 0, 0))],
        scratch_shapes=[pltpu.VMEM((2, pages, PAGE_SIZE, A_KV_LORA), f32),
                        pltpu.VMEM((2, pages, A_ROPE, PAGE_SIZE), f32),
                        pltpu.VMEM((2, pages, C_WIDTH, PAGE_SIZE), f32),
                        pltpu.SemaphoreType.DMA((2, 3)),
                        pltpu.VMEM((A_HEADS, 1), f32), pltpu.VMEM((A_HEADS, 1), f32),
                        pltpu.VMEM((A_HEADS, A_KV_LORA), f32),
                        pltpu.VMEM((C_HEADS * nbp, PAGE_SIZE), f32)],
    )
    return pl.pallas_call(
        functools.partial(_paged_kernel, layer=layer, pages=pages, cps=cps),
        grid_spec=grid_spec,
        out_shape=[jax.ShapeDtypeStruct((DB, A_HEADS, A_KV_LORA), f32),
                   jax.ShapeDtypeStruct((DB, SUBLANES, LANES), jnp.int32)],
        compiler_params=_params(("arbitrary",)),
        name="sample_paged",
    )(page_table, qa, qb, latn, krn, qcol, cache_lat, cache_krt, cache_kt)


def _value_up_kernel(o_ref, w_ref, y_ref):
    y_ref[...] = _dot(o_ref[...], w_ref[...], HI).astype(y_ref.dtype)


def mla_value_up(olat, wuv_bd):
    return pl.pallas_call(_value_up_kernel,
                          out_shape=jax.ShapeDtypeStruct((olat.shape[0], wuv_bd.shape[1]), bf16),
                          compiler_params=_params(None), name="mla_value_up")(olat, wuv_bd)


def _mlstm_step_kernel(zm_ref, cs_ref, cw_ref, cb_ref, wqt_ref, wkt_ref, wvt_ref, bi_ref, bf_ref,
                       c0_ref, n0_ref, m0_ref, h_ref, c1_ref, n1_ref, m1_ref, qt_s, kt_s):
    W = M_WIDTH
    zm = zm_ref[...]
    u = zm[:, :W]
    acc = u * cw_ref[M_CONV - 1:M_CONV, :]
    for j in range(M_CONV - 1):
        acc = acc + cs_ref[j] * cw_ref[j:j + 1, :]
    uc = _silu(acc + cb_ref[...])
    qt_s[...] = _nt(wqt_ref[...], uc, HI)
    kt_s[...] = _nt(wkt_ref[...], uc, HI)
    vt = _nt(wvt_ref[...], u, HI)
    gate_t = jax.nn.sigmoid(zm[:, W:2 * W]).T
    i_t = (zm[:, 2 * W:2 * W + LANES] + bi_ref[...]).T[:SUBLANES, :]
    f_t = _log_sigmoid(zm[:, 2 * W + LANES:] + bf_ref[...]).T[:SUBLANES, :]
    outs = []
    for h in range(M_HEADS):
        rows = slice(h * M_DH, (h + 1) * M_DH)
        i_h = i_t[h:h + 1, :]
        a = f_t[h:h + 1, :] + m0_ref[h:h + 1, :]
        mt = jnp.maximum(a, i_h)
        w_old = jnp.exp(a - mt)
        w_tok = jnp.exp(i_h - mt)
        qh = qt_s[rows, :]
        kh = kt_s[rows, :]
        vh = vt[rows, :]
        s = jnp.sum(qh * kh, axis=0, keepdims=True) * w_tok
        qn = jnp.sum(qh * n0_ref[h], axis=0, keepdims=True)

        def body(d, qc, h=h, w_old=w_old, w_tok=w_tok, vh=vh):
            c0 = c0_ref[h, d]
            k_row = kt_s[pl.ds(h * M_DH + d, 1), :]
            q_row = qt_s[pl.ds(h * M_DH + d, 1), :]
            c1_ref[h, d] = w_old * c0 + (w_tok * k_row) * vh
            return qc + q_row * c0

        qc = lax.fori_loop(0, M_DH, body, jnp.zeros((M_DH, zm.shape[0]), f32))
        num = w_old * qc + s * vh
        den = w_old * qn + s
        outs.append(num / jnp.maximum(jnp.abs(den), jnp.exp(-mt)) * gate_t[rows, :])
        n1_ref[h] = w_old * n0_ref[h] + w_tok * kh
        m1_ref[h:h + 1, :] = mt
    h_ref[...] = jnp.concatenate(outs, axis=0).T.astype(h_ref.dtype)


def mlstm_sample_step(zm, conv_t, cw, cb, wqt, wkt, wvt, bi, bfg, c0, n0, m0):
    DB = zm.shape[0]
    shapes = [jax.ShapeDtypeStruct((DB, M_WIDTH), bf16), jax.ShapeDtypeStruct(c0.shape, f32),
              jax.ShapeDtypeStruct(n0.shape, f32), jax.ShapeDtypeStruct(m0.shape, f32)]
    return pl.pallas_call(
        _mlstm_step_kernel, out_shape=shapes,
        scratch_shapes=[pltpu.VMEM((M_WIDTH, DB), f32), pltpu.VMEM((M_WIDTH, DB), f32)],
        compiler_params=_params(None), name="mlstm_sample_step",
    )(zm, conv_t, cw, cb, wqt, wkt, wvt, bi, bfg, c0, n0, m0)


def _moba_decode_kernel(pt_ref, idx_ref, tab_ref, q_ref, kn_ref, vn_ref, kt_hbm, vt_hbm, o_ref,
                        k_buf, v_buf, sems, *, layer, n_sel, past):
    b = pl.program_id(0)
    slot = b % 2
    ppb = C_BLOCK // PAGE_SIZE
    scale = C_DH ** -0.5

    def copies(seq, sl):
        out = []
        for h in range(C_HEADS):
            for j in range(n_sel):
                blk = idx_ref[(seq * C_HEADS + h) * C_TOPK + j // ppb]
                pg = pt_ref[seq, blk * ppb + j % ppb]
                rows = pl.ds(h * C_DH, C_DH)
                out.append(pltpu.make_async_copy(kt_hbm.at[layer, pg, rows], k_buf.at[sl, h * n_sel + j], sems.at[sl, 0]))
                out.append(pltpu.make_async_copy(vt_hbm.at[layer, pg, rows], v_buf.at[sl, h * n_sel + j], sems.at[sl, 1]))
        return out

    @pl.when(b == 0)
    def _():
        for cp in copies(0, 0):
            cp.start()

    @pl.when(b + 1 < pl.num_programs(0))
    def _():
        for cp in copies(b + 1, 1 - slot):
            cp.start()

    for cp in copies(b, slot):
        cp.wait()

    t = lax.broadcasted_iota(jnp.int32, (1, C_BLOCK), 1)
    heads = range(C_HEADS)
    scores = []
    for h in heads:
        kt = jnp.concatenate([k_buf[slot, h * n_sel + j] for j in range(n_sel)], axis=1).astype(bf16)
        q8 = jnp.broadcast_to((q_ref[h] * scale).astype(bf16), (SUBLANES, C_DH))
        scores.append(_dot(q8, kt)[:1, :])
    probs, p_selfs, ls = [], [], []
    for h in heads:
        q = q_ref[h]
        bias = [_rel_bias(past - (idx_ref[(b * C_HEADS + h) * C_TOPK + j] * C_BLOCK + t), tab_ref, h)
                for j in range(n_sel // ppb)]
        s = scores[h] + jnp.concatenate(bias, axis=1)
        s_self = jnp.sum(q * kn_ref[h], -1, keepdims=True) * scale + tab_ref[h * REL_BUCKETS]
        m = jnp.maximum(jnp.max(s, -1, keepdims=True), s_self)
        p = jnp.exp(s - m)
        p_self = jnp.exp(s_self - m)
        ls.append(jnp.sum(p, -1, keepdims=True) + p_self)
        p_selfs.append(p_self)
        probs.append(jnp.broadcast_to(p.astype(bf16), (SUBLANES, p.shape[1])))
    for h in heads:
        vt = jnp.concatenate([v_buf[slot, h * n_sel + j] for j in range(n_sel)], axis=1).astype(bf16)
        pv = _nt(probs[h], vt)[:1, :]
        o_ref[h] = (pv + p_selfs[h] * vn_ref[h]) / ls[h]


def moba_decode(page_table, idx_flat, tab, q, kn, vn, cache_kt, cache_vt, layer):
    DB, n_pages = page_table.shape
    ppb = C_BLOCK // PAGE_SIZE
    n_sel = C_TOPK * ppb
    past = n_pages * PAGE_SIZE
    seq = pl.BlockSpec((None, C_HEADS, 1, C_DH), lambda b, pt, idx: (b, 0, 0, 0))
    hbm = pl.BlockSpec(memory_space=pl.ANY)
    grid_spec = pltpu.PrefetchScalarGridSpec(
        num_scalar_prefetch=2,
        grid=(DB,),
        in_specs=[pl.BlockSpec(memory_space=pltpu.SMEM), seq, seq, seq, hbm, hbm],
        out_specs=seq,
        scratch_shapes=[pltpu.VMEM((2, C_HEADS * n_sel, C_DH, PAGE_SIZE), f32),
                        pltpu.VMEM((2, C_HEADS * n_sel, C_DH, PAGE_SIZE), f32),
                        pltpu.SemaphoreType.DMA((2, 2))],
    )
    return pl.pallas_call(
        functools.partial(_moba_decode_kernel, layer=layer, n_sel=n_sel, past=past),
        grid_spec=grid_spec,
        out_shape=jax.ShapeDtypeStruct((DB, C_HEADS, 1, C_DH), f32),
        compiler_params=_params(("arbitrary",)),
        name="moba_decode",
    )(page_table, idx_flat, tab, q, kn, vn, cache_kt, cache_vt)


def _pack_w_in(w):
    d = w.shape[0]
    z = lambda n: jnp.zeros((d, n), w.dtype)
    o = 0
    parts = {}
    for name, size in (("cq", A_Q_LORA), ("ckv", A_KV_LORA), ("kr", A_ROPE), ("u", M_WIDTH), ("o", M_WIDTH),
                       ("i", M_HEADS), ("f", M_HEADS), ("qkv", 3 * C_WIDTH), ("g", 3 * D_MODEL)):
        parts[name] = w[:, o:o + size]
        o += size
    kr = parts["kr"]
    half = A_ROPE // 2
    pad = LANES - A_NOPE - A_ROPE
    krp = jnp.concatenate([z(A_NOPE), kr, z(pad)], axis=1)
    krs = jnp.concatenate([z(A_NOPE), kr[:, half:], kr[:, :half], z(pad)], axis=1)
    return jnp.concatenate([parts["cq"], parts["ckv"], krp, krs, parts["u"], parts["o"],
                            parts["i"], z(LANES - M_HEADS), parts["f"], z(LANES - M_HEADS),
                            parts["qkv"], parts["g"]], axis=1)


def _pack_w_q(w):
    r = w.shape[0]
    w3 = w.reshape(r, A_HEADS, A_NOPE + A_ROPE)
    half = A_ROPE // 2
    pad = jnp.zeros((r, A_HEADS, LANES - A_NOPE - A_ROPE), w.dtype)
    nope, rope = w3[..., :A_NOPE], w3[..., A_NOPE:]
    w1 = jnp.concatenate([nope, rope, pad], axis=-1)
    w2 = jnp.concatenate([jnp.zeros_like(nope), rope[..., half:], rope[..., :half], pad], axis=-1)
    return w1.reshape(r, -1), w2.reshape(r, -1)


def _rope_tables(pos):
    inv = ROPE_THETA ** (-jnp.arange(0, A_ROPE, 2, dtype=f32) / A_ROPE)
    ang = pos.astype(f32)[:, None] * inv[None, :]
    cos, sin = jnp.cos(ang), jnp.sin(ang)
    n = pos.shape[0]
    pad = jnp.zeros((n, LANES - A_NOPE - A_ROPE), f32)
    ct = jnp.concatenate([jnp.ones((n, A_NOPE), f32), cos, cos, pad], axis=1)
    st = jnp.concatenate([jnp.zeros((n, A_NOPE), f32), -sin, sin, pad], axis=1)
    return ct, st


def _block_diag(w):
    hN, a, b = w.shape
    eye = jnp.eye(hN, dtype=w.dtype)
    return (eye[:, None, :, None] * w[:, :, None, :]).reshape(hN * a, hN * b)


def _lane_pad(v, width=LANES):
    return jnp.pad(v, ((0, 0), (0, width - v.shape[-1])))


def kernel(x_prompt, x_sample, cache_mla_latent, cache_mla_krope, cache_moba_k, cache_moba_v, state_mlstm_C, state_mlstm_n, state_mlstm_m, state_mlstm_conv, page_table, c_prompt, c_sample, rel_table, w_ada, b_ada, w_in, g_q_norm, w_q_up, g_kv_norm, w_kv_up, conv_w, conv_b, w_mq, w_mk, w_mv, b_i, b_f, w_br_a, w_br_m, w_br_c, w_out, ln1_g, ln1_b, w_ff_gate, w_ff_up, w_ff_down, ln2_g, ln2_b):
    B, S, d = x_prompt.shape
    DB, T, _ = x_sample.shape
    depth = w_in.shape[0]
    n_pool = cache_mla_latent.shape[1]
    n_pages = page_table.shape[1]
    past = n_pages * PAGE_SIZE
    assert T == 1 and past % C_BLOCK == 0 and past // C_BLOCK >= C_TOPK and S % C_BLOCK == 0
    alpha = (2 * depth) ** 0.25
    dff = w_ff_gate.shape[2]

    mods_all = ada_mods(jnp.concatenate([c_prompt, c_sample], axis=0), w_ada, b_ada)
    ct_p, st_p = _rope_tables(jnp.arange(S, dtype=jnp.int32))
    ct_s, st_s = _rope_tables(jnp.full((DB,), past, jnp.int32))
    tab = rel_table.T.reshape(-1)
    cache_kt = jnp.transpose(cache_moba_k, (0, 1, 3, 4, 2)).reshape(depth, n_pool, C_WIDTH, PAGE_SIZE)
    cache_vt = jnp.transpose(cache_moba_v, (0, 1, 3, 4, 2)).reshape(depth, n_pool, C_WIDTH, PAGE_SIZE)
    cache_krt = jnp.transpose(cache_mla_krope, (0, 1, 3, 2))
    xs = x_sample.reshape(1, DB, d)
    xp = x_prompt

    new_p = [[] for _ in range(8)]
    new_s = [[] for _ in range(8)]
    for l in range(depth):
        w_pack = _pack_w_in(w_in[l]).astype(bf16)
        wq1, wq2 = _pack_w_q(w_q_up[l])
        w_uk = w_kv_up[l][..., :A_NOPE]
        w_uv = w_kv_up[l][..., A_NOPE:]
        wuk = jnp.concatenate([w_uk, jnp.zeros_like(w_uk)], axis=-1).reshape(A_KV_LORA, -1)
        wuv = w_uv.reshape(A_KV_LORA, -1)
        gq = g_q_norm[l][None, :]
        gkv = g_kv_norm[l][None, :]
        wq_bd = _block_diag(w_mq[l])
        wk_bd = _block_diag(w_mk[l]) * (M_DH ** -0.5)
        wv_bd = _block_diag(w_mv[l])
        bi = _lane_pad(b_i[l][None, :])
        bfg = _lane_pad(b_f[l][None, :])
        cw = conv_w[l]
        cb = conv_b[l][None, :]
        tail_w = (w_br_a[l].astype(bf16), w_br_m[l].astype(bf16), w_br_c[l].astype(bf16), w_out[l].astype(bf16),
                  ln1_g[l][None, :], ln1_b[l][None, :], ln2_g[l][None, :], ln2_b[l][None, :],
                  w_ff_gate[l].astype(bf16), w_ff_up[l].astype(bf16), w_ff_down[l].astype(bf16))

        mods_p = mods_all[l, :B].reshape(B, 1, 6 * d)
        za, zm, qc, kc, vc, sg = in_proj(xp, mods_p, w_pack, tm=512)
        qh, kh, vh, lat, krope = mla_prep(za, ct_p, st_p, gq, gkv, wq1.astype(bf16), wq2.astype(bf16),
                                          wuk.astype(bf16), wuv.astype(bf16), tm=512, qk_dtype=bf16)
        ya = mla_attn(qh, kh, vh, tq=512)
        ym, c_st, n_st, m_st = mlstm_prompt(zm, cw, cb, wq_bd.astype(bf16), wk_bd.astype(bf16),
                                            wk_bd.T.astype(bf16), wv_bd.astype(bf16), bi, bfg, L=256)
        yc = moba_prompt(qc, kc, vc, tab)
        xp_new = layer_out(xp, ya, ym, yc, sg, mods_p, *tail_w, tm=512, tf=dff // 2, alpha=alpha)
        C1 = jnp.stack([c_st[:, h * M_DH:(h + 1) * M_DH, h * M_DH:(h + 1) * M_DH] for h in range(M_HEADS)], axis=1)
        n1 = jnp.stack([n_st[:, h * M_DH:(h + 1) * M_DH, h] for h in range(M_HEADS)], axis=1)
        m1 = m_st[:, :M_HEADS, 0]
        u_p = zm[:, :, :M_WIDTH]
        conv_p = jnp.pad(u_p, ((0, 0), (M_CONV - 1, 0), (0, 0)))[:, -(M_CONV - 1):]
        for lst, val in zip(new_p, (lat, krope, kc.reshape(B, S, C_HEADS, C_DH), vc.reshape(B, S, C_HEADS, C_DH),
                                    C1, n1, m1, conv_p)):
            lst.append(val)
        xp = xp_new

        mods_s = mods_all[l, B:].reshape(1, DB, 6 * d)
        za, zm, qc, kc, vc, sg = in_proj(xs, mods_s, w_pack, tm=DB)
        qh, _, _, lat, krope = mla_prep(za, ct_s, st_s, gq, gkv, wq1, wq2, wuk, wuv, tm=DB, qk_dtype=f32)
        wabs = jnp.concatenate([jnp.transpose(w_uk, (1, 2, 0)),
                                jnp.zeros((A_HEADS, HEAD_BLK - A_NOPE, A_KV_LORA), f32)], axis=1)
        qh2 = qh.reshape(DB, A_HEADS * HEAD_BLK)
        qabs = mla_absorb(qh2, wabs)
        olat, idx = sample_paged(page_table, qabs.reshape(DB, A_HEADS, A_KV_LORA), qh2.reshape(DB, A_HEADS, HEAD_BLK),
                                 lat.reshape(DB, 1, A_KV_LORA), krope.reshape(DB, 1, A_ROPE),
                                 qc.reshape(DB, C_WIDTH, 1), cache_mla_latent, cache_krt, cache_kt, l, pages=64)
        wuv_bd = _block_diag(jnp.transpose(w_uv, (1, 0, 2)))
        ya = mla_value_up(olat.reshape(DB, A_HEADS * A_KV_LORA), wuv_bd)

        zm2 = zm.reshape(DB, ZM_W)
        conv_state = state_mlstm_conv[l]
        ym, C1t, n1t, m1t = mlstm_sample_step(
            zm2, jnp.transpose(conv_state, (1, 0, 2)), cw, cb, wq_bd.T, wk_bd.T, wv_bd.T, bi, bfg,
            jnp.transpose(state_mlstm_C[l], (1, 2, 3, 0)), jnp.transpose(state_mlstm_n[l], (1, 2, 0)),
            jnp.transpose(state_mlstm_m[l], (1, 0)))
        C1 = jnp.transpose(C1t, (3, 0, 1, 2))
        n1 = jnp.transpose(n1t, (2, 0, 1))
        m1 = jnp.transpose(m1t, (1, 0))
        u_s = zm2[:, :M_WIDTH]
        conv_s = jnp.concatenate([conv_state[:, 1:], u_s[:, None, :]], axis=1)

        idx_flat = idx[:, :C_HEADS, :C_TOPK].reshape(-1)
        per_head = lambda a: a.reshape(DB, C_HEADS, 1, C_DH)
        yc = moba_decode(page_table, idx_flat, tab, per_head(qc), per_head(kc), per_head(vc), cache_kt, cache_vt, l)
        xs_new = layer_out(xs, ya.reshape(1, DB, -1), ym.reshape(1, DB, -1), yc.reshape(1, DB, -1).astype(bf16),
                           sg, mods_s, *tail_w, tm=DB, tf=dff // 2, alpha=alpha)
        for lst, val in zip(new_s, (lat.reshape(DB, 1, A_KV_LORA), krope.reshape(DB, 1, A_ROPE),
                                    kc.reshape(DB, 1, C_HEADS, C_DH), vc.reshape(DB, 1, C_HEADS, C_DH),
                                    C1.reshape(DB, M_HEADS, M_DH, M_DH), n1.reshape(DB, M_HEADS, M_DH),
                                    m1.reshape(DB, M_HEADS), conv_s)):
            lst.append(val)
        xs = xs_new

    outs_p = [jnp.stack(a) for a in new_p]
    outs_s = [jnp.stack(a) for a in new_s]
    return (xp, xs.reshape(DB, T, d), *outs_p, *outs_s)
```

```python
import functools
import math

import jax
import jax.numpy as jnp
from jax import lax
from jax.experimental import pallas as pl
from jax.experimental.pallas import tpu as pltpu

f32 = jnp.float32
bf16 = jnp.bfloat16
HI = lax.Precision.HIGHEST

D_MODEL = 1024
PAGE_SIZE = 128
A_HEADS = 8
A_NOPE = 64
A_ROPE = 32
A_VDIM = 64
A_Q_LORA = 256
A_KV_LORA = 128
ROPE_THETA = 10000.0
M_HEADS = 4
M_DH = 64
M_WIDTH = M_HEADS * M_DH
M_CONV = 4
C_HEADS = 4
C_DH = 64
C_WIDTH = C_HEADS * C_DH
C_BLOCK = 256
C_TOPK = 3
REL_BUCKETS = 32
REL_MAX_DIST = 128
LN_EPS = 1e-5
RMS_EPS = 1e-6
LANES = 128
SUBLANES = 8
VMEM_LIMIT = 56 * 1024 * 1024
NEG = -1e30

ZA_W = A_Q_LORA + A_KV_LORA + 2 * LANES
ZM_W = 2 * M_WIDTH + 2 * LANES
ZC_W = 3 * C_WIDTH
ZG_W = 3 * D_MODEL
PACK_W = ZA_W + ZM_W + ZC_W + ZG_W
HEAD_BLK = LANES


def _rel_thresholds():
    exact = REL_BUCKETS // 2
    out = []
    for k in range(1, REL_BUCKETS - exact):
        n = exact
        while int(math.log(n / exact) / math.log(REL_MAX_DIST / exact) * (REL_BUCKETS - exact)) < k:
            n += 1
        out.append(n)
    return tuple(out)


REL_THRESH = _rel_thresholds()


def _nt(a, b, precision=None):
    return lax.dot_general(a, b, (((1,), (1,)), ((), ())), precision=precision, preferred_element_type=f32)


def _tn(a, b):
    return lax.dot_general(a, b, (((0,), (0,)), ((), ())), preferred_element_type=f32)


def _dot(a, b, precision=None):
    return jnp.dot(a, b, precision=precision, preferred_element_type=f32)


def _silu(x):
    return x * jax.nn.sigmoid(x)


def _layer_norm(x, g, b):
    mu = jnp.mean(x, -1, keepdims=True)
    xc = x - mu
    var = jnp.mean(xc * xc, -1, keepdims=True)
    return xc * lax.rsqrt(var + LN_EPS) * g + b


def _rms_norm(x, g):
    return x * lax.rsqrt(jnp.mean(x * x, -1, keepdims=True) + RMS_EPS) * g


def _rel_bias(dist, tab_ref, h):
    n = jnp.maximum(dist, 0)
    exact = REL_BUCKETS // 2
    large = jnp.full(n.shape, exact, jnp.int32)
    for t in REL_THRESH:
        large = large + (n >= t).astype(jnp.int32)
    bucket = jnp.where(n < exact, n, large)
    val = jnp.full(n.shape, tab_ref[h * REL_BUCKETS + REL_BUCKETS - 1], f32)
    for b in range(REL_BUCKETS - 1):
        val = jnp.where(bucket == b, tab_ref[h * REL_BUCKETS + b], val)
    return val


def _params(sem):
    return pltpu.CompilerParams(dimension_semantics=sem, vmem_limit_bytes=VMEM_LIMIT)


def _ada_kernel(c_ref, w_ref, b_ref, o_ref):
    c = c_ref[...]
    o_ref[...] = _dot(_silu(c), w_ref[...], HI) + b_ref[...]


def ada_mods(c_all, w_ada, b_ada):
    depth, d, n = w_ada.shape
    rows = c_all.shape[0]
    tn = n // 4
    return pl.pallas_call(
        _ada_kernel,
        grid=(depth, n // tn),
        in_specs=[
            pl.BlockSpec((rows, d), lambda l, j: (0, 0)),
            pl.BlockSpec((None, d, tn), lambda l, j: (l, 0, j)),
            pl.BlockSpec((None, 1, tn), lambda l, j: (l, 0, j)),
        ],
        out_specs=pl.BlockSpec((None, rows, tn), lambda l, j: (l, 0, j)),
        out_shape=jax.ShapeDtypeStruct((depth, rows, n), f32),
        compiler_params=_params(("arbitrary", "arbitrary")),
        name="ada_mods",
    )(c_all, w_ada, b_ada.reshape(depth, 1, n))


def _mod_spec(mods, tm, col):
    per_row = mods.shape[1] != 1
    rows = tm if per_row else 1
    return pl.BlockSpec((1, rows, D_MODEL), lambda b, i, *_: (b, i if per_row else 0, col))


def _in_proj_kernel(x_ref, sh_ref, sc_ref, w_ref, za_ref, zm_ref, q_ref, k_ref, v_ref, sg_ref):
    h = (x_ref[0] * (1.0 + sc_ref[0]) + sh_ref[0]).astype(w_ref.dtype)

    def mm(c0, c1):
        return _dot(h, w_ref[:, c0:c1])

    c = 0
    za_ref[0] = mm(c, c + ZA_W)
    c += ZA_W
    zm_ref[0] = mm(c, c + ZM_W)
    c += ZM_W
    q_ref[0] = mm(c, c + C_WIDTH)
    k_ref[0] = mm(c + C_WIDTH, c + 2 * C_WIDTH)
    v_ref[0] = mm(c + 2 * C_WIDTH, c + 3 * C_WIDTH)
    c += ZC_W
    sg_ref[0] = jax.nn.sigmoid(mm(c, c + ZG_W)).astype(sg_ref.dtype)


def in_proj(x, mods, w_pack, tm):
    B, S, d = x.shape
    widths = (ZA_W, ZM_W, C_WIDTH, C_WIDTH, C_WIDTH, ZG_W)
    dts = (f32, f32, f32, f32, f32, bf16)
    return pl.pallas_call(
        _in_proj_kernel,
        grid=(B, S // tm),
        in_specs=[
            pl.BlockSpec((1, tm, d), lambda b, i: (b, i, 0)),
            _mod_spec(mods, tm, 0),
            _mod_spec(mods, tm, 1),
            pl.BlockSpec((d, PACK_W), lambda b, i: (0, 0)),
        ],
        out_specs=[pl.BlockSpec((1, tm, w), lambda b, i: (b, i, 0)) for w in widths],
        out_shape=[jax.ShapeDtypeStruct((B, S, w), t) for w, t in zip(widths, dts)],
        compiler_params=_params(("arbitrary", "arbitrary")),
        name="in_proj",
    )(x, mods, mods, w_pack)


def _mla_prep_kernel(za_ref, ct_ref, st_ref, gq_ref, gkv_ref, wq1_ref, wq2_ref, wuk_ref, wuv_ref,
                     qh_ref, kh_ref, vh_ref, lat_ref, kr_ref, *, prec):
    za = za_ref[0]
    cq = za[:, :A_Q_LORA]
    ckv = za[:, A_Q_LORA:A_Q_LORA + A_KV_LORA]
    krp = za[:, A_Q_LORA + A_KV_LORA:A_Q_LORA + A_KV_LORA + LANES]
    krs = za[:, A_Q_LORA + A_KV_LORA + LANES:]
    ct = ct_ref[...]
    st = st_ref[...]
    scale = (A_NOPE + A_ROPE) ** -0.5
    cqn = _rms_norm(cq, gq_ref[...]).astype(wq1_ref.dtype)
    q1 = _dot(cqn, wq1_ref[...], prec)
    q2 = _dot(cqn, wq2_ref[...], prec)
    lat = _rms_norm(ckv, gkv_ref[...])
    lat_ref[0] = lat
    kro = krp * ct + krs * st
    kr_ref[0] = kro[:, A_NOPE:A_NOPE + A_ROPE]
    latc = lat.astype(wuk_ref.dtype)
    kn = _dot(latc, wuk_ref[...], prec)
    for h in range(A_HEADS):
        blk = slice(h * HEAD_BLK, (h + 1) * HEAD_BLK)
        qh_ref[0, :, blk] = ((q1[:, blk] * ct + q2[:, blk] * st) * scale).astype(qh_ref.dtype)
        kh_ref[0, :, blk] = (kn[:, blk] + kro).astype(kh_ref.dtype)
    vh_ref[0] = _dot(latc, wuv_ref[...], prec).astype(vh_ref.dtype)


def mla_prep(za, ctab, stab, gq, gkv, wq1, wq2, wuk, wuv, tm, qk_dtype):
    B, S, _ = za.shape
    full = lambda a: pl.BlockSpec(a.shape, lambda b, i: (0,) * a.ndim)
    tok = lambda w: pl.BlockSpec((1, tm, w), lambda b, i: (b, i, 0))
    widths = (A_HEADS * HEAD_BLK, A_HEADS * HEAD_BLK, A_HEADS * A_VDIM, A_KV_LORA, A_ROPE)
    dts = (qk_dtype, qk_dtype, qk_dtype, f32, f32)
    prec = HI if qk_dtype == f32 else None
    return pl.pallas_call(
        functools.partial(_mla_prep_kernel, prec=prec),
        grid=(B, S // tm),
        in_specs=[tok(ZA_W),
                  pl.BlockSpec((tm, LANES), lambda b, i: (i, 0)),
                  pl.BlockSpec((tm, LANES), lambda b, i: (i, 0)),
                  full(gq), full(gkv), full(wq1), full(wq2), full(wuk), full(wuv)],
        out_specs=[tok(w) for w in widths],
        out_shape=[jax.ShapeDtypeStruct((B, S, w), t) for w, t in zip(widths, dts)],
        compiler_params=_params(("arbitrary", "arbitrary")),
        name="mla_prep",
    )(za, ctab, stab, gq, gkv, wq1, wq2, wuk, wuv)


def _mla_attn_kernel(q_ref, k_ref, v_ref, o_ref, *, tq):
    S = q_ref.shape[1]
    lane = lax.broadcasted_iota(jnp.int32, (tq, 2 * A_VDIM), 1)
    for t in range(S // tq):
        kv_len = (t + 1) * tq
        row = t * tq + lax.broadcasted_iota(jnp.int32, (tq, kv_len), 0)
        col = lax.broadcasted_iota(jnp.int32, (tq, kv_len), 1)
        causal = col <= row
        vpair = v_ref[0, :kv_len, :]
        outs = []
        for hh in range(2):
            blk = slice(hh * HEAD_BLK, (hh + 1) * HEAD_BLK)
            s = _nt(q_ref[0, t * tq:(t + 1) * tq, blk], k_ref[0, :kv_len, blk])
            s = jnp.where(causal, s, -jnp.inf)
            m = jnp.max(s, -1, keepdims=True)
            p = jnp.exp(s - m)
            l = jnp.sum(p, -1, keepdims=True)
            outs.append(_dot(p.astype(vpair.dtype), vpair) / l)
        o_ref[0, t * tq:(t + 1) * tq, :] = jnp.where(lane < A_VDIM, outs[0], outs[1]).astype(o_ref.dtype)


def mla_attn(qh, kh, vh, tq):
    B, S, _ = qh.shape
    pairs = A_HEADS // 2
    return pl.pallas_call(
        functools.partial(_mla_attn_kernel, tq=tq),
        grid=(B, pairs),
        in_specs=[pl.BlockSpec((1, S, 2 * HEAD_BLK), lambda b, p: (b, 0, p)),
                  pl.BlockSpec((1, S, 2 * HEAD_BLK), lambda b, p: (b, 0, p)),
                  pl.BlockSpec((1, S, 2 * A_VDIM), lambda b, p: (b, 0, p))],
        out_specs=pl.BlockSpec((1, S, 2 * A_VDIM), lambda b, p: (b, 0, p)),
        out_shape=jax.ShapeDtypeStruct((B, S, A_HEADS * A_VDIM), bf16),
        compiler_params=_params(("arbitrary", "arbitrary")),
        name="mla_attn",
    )(qh, kh, vh)


def _log_sigmoid(x):
    return jnp.minimum(x, 0.0) - jnp.log1p(jnp.exp(-jnp.abs(x)))


def _mlstm_kernel(zm_ref, cw_ref, cb_ref, wq_ref, wk_ref, wkt_ref, wv_ref, bi_ref, bf_ref,
                  ym_ref, c_out_ref, n_out_ref, m_out_ref,
                  c_st, n_st, m_row, m_col, tail):
    L = zm_ref.shape[1]
    W = M_WIDTH
    c = pl.program_id(1)

    @pl.when(c == 0)
    def _():
        c_st[...] = jnp.zeros_like(c_st)
        n_st[...] = jnp.zeros_like(n_st)
        m_row[...] = jnp.zeros_like(m_row)
        m_col[...] = jnp.zeros_like(m_col)
        tail[...] = jnp.zeros_like(tail)

    zm = zm_ref[0]
    u = zm[:, :W]
    o_raw = zm[:, W:2 * W]
    ig = zm[:, 2 * W:2 * W + LANES]
    fg = zm[:, 2 * W + LANES:]

    rows = lax.broadcasted_iota(jnp.int32, (L, W), 0)
    tl = tail[...]
    acc = u * cw_ref[M_CONV - 1:M_CONV, :]
    for j in range(1, M_CONV):
        ru = pltpu.roll(u, j, 0)
        rt = jnp.concatenate([pltpu.roll(tl, j, 0), jnp.zeros((L - SUBLANES, W), f32)], axis=0)
        acc = acc + jnp.where(rows < j, rt, ru) * cw_ref[M_CONV - 1 - j:M_CONV - j, :]
    tail[...] = u[L - SUBLANES:, :]
    uc = _silu(acc + cb_ref[...]).astype(bf16)
    ub = u.astype(bf16)

    q = _dot(uc, wq_ref[...])
    k = _dot(uc, wk_ref[...])
    kt = _nt(wkt_ref[...], uc)
    v = _dot(ub, wv_ref[...])
    qb = q.astype(bf16)
    kb = k.astype(bf16)
    vb = v.astype(bf16)

    i_c = ig + bi_ref[...]
    f_c = _log_sigmoid(fg + bf_ref[...])
    t_i = lax.broadcasted_iota(jnp.int32, (L, L), 0)
    s_i = lax.broadcasted_iota(jnp.int32, (L, L), 1)
    causal = s_i <= t_i
    tri = causal.astype(f32)
    b_c = _dot(tri, f_c, HI)
    b_r = b_c.T[:SUBLANES, :]
    i_r = i_c.T[:SUBLANES, :]
    a_c = b_c + m_row[...]

    inter = _dot(qb, c_st[...].astype(bf16))
    qn = _dot(qb, n_st[...].astype(bf16))
    lane_head = lax.broadcasted_iota(jnp.int32, (1, W), 1) // M_DH
    intra = jnp.zeros((L, W), f32)
    w_full = jnp.zeros((L, W), f32)
    r_full = jnp.zeros((L, W), f32)
    for h in range(M_HEADS):
        hm = lane_head == h
        dmat = jnp.where(causal, b_c[:, h:h + 1] - b_r[h:h + 1, :] + i_r[h:h + 1, :], -jnp.inf)
        a_h = a_c[:, h:h + 1]
        mt = jnp.maximum(a_h, jnp.max(dmat, -1, keepdims=True))
        pm = jnp.exp(dmat - mt)
        s = _nt(jnp.where(hm, qb, jnp.zeros_like(qb)), kb) * pm
        w_h = jnp.exp(a_h - mt)
        den = w_h * qn[:, h:h + 1] + jnp.sum(s, -1, keepdims=True)
        r_h = 1.0 / jnp.maximum(jnp.abs(den), jnp.exp(-mt))
        intra = intra + _dot(s.astype(bf16), jnp.where(hm, vb, jnp.zeros_like(vb)))
        w_full = jnp.where(hm, w_h, w_full)
        r_full = jnp.where(hm, r_h, r_full)
    hout = (w_full * inter + intra) * r_full
    ym_ref[0] = (hout * jax.nn.sigmoid(o_raw)).astype(ym_ref.dtype)

    b_end = b_r[:, L - 1:L]
    g_r = b_end - b_r + i_r
    mc = m_col[...][:, :1]
    m_new = jnp.maximum(b_end + mc, jnp.max(g_r, -1, keepdims=True))
    w_old = jnp.exp(b_end + mc - m_new)
    w_tok = jnp.exp(g_r - m_new)
    w_tok_full = jnp.concatenate([jnp.broadcast_to(w_tok[h:h + 1, :], (M_DH, L)) for h in range(M_HEADS)], axis=0)
    w_old_full = jnp.concatenate([jnp.broadcast_to(w_old[h:h + 1, :], (M_DH, 1)) for h in range(M_HEADS)], axis=0)
    ktw = (kt * w_tok_full).astype(bf16)
    row_head = lax.broadcasted_iota(jnp.int32, (W, W), 0) // M_DH
    col_head = lax.broadcasted_iota(jnp.int32, (W, W), 1) // M_DH
    c_st[...] = w_old_full * c_st[...] + jnp.where(row_head == col_head, _dot(ktw, vb), 0.0)
    n_rows = lax.broadcasted_iota(jnp.int32, (W, LANES), 0) // M_DH
    n_cols = lax.broadcasted_iota(jnp.int32, (W, LANES), 1)
    ksum = _dot(ktw, jnp.ones((L, LANES), bf16))
    n_st[...] = w_old_full * n_st[...] + jnp.where(n_rows == n_cols, ksum, 0.0)
    eye = (lax.broadcasted_iota(jnp.int32, (SUBLANES, LANES), 0)
           == lax.broadcasted_iota(jnp.int32, (SUBLANES, LANES), 1)).astype(f32)
    m_row[...] = jnp.sum(eye * m_new, axis=0, keepdims=True)
    m_col[...] = jnp.broadcast_to(m_new, (SUBLANES, LANES))

    @pl.when(c == pl.num_programs(1) - 1)
    def _():
        c_out_ref[0] = c_st[...]
        n_out_ref[0] = n_st[...]
        m_out_ref[0] = m_col[...]


def mlstm_prompt(zm, cw, cb, wq, wk, wkt, wv, bi, bfg, L):
    B, S, _ = zm.shape
    W = M_WIDTH
    full = lambda a: pl.BlockSpec(a.shape, lambda b, c: (0,) * a.ndim)
    return pl.pallas_call(
        _mlstm_kernel,
        grid=(B, S // L),
        in_specs=[pl.BlockSpec((1, L, ZM_W), lambda b, c: (b, c, 0)),
                  full(cw), full(cb), full(wq), full(wk), full(wkt), full(wv), full(bi), full(bfg)],
        out_specs=[pl.BlockSpec((1, L, W), lambda b, c: (b, c, 0)),
                   pl.BlockSpec((1, W, W), lambda b, c: (b, 0, 0)),
                   pl.BlockSpec((1, W, LANES), lambda b, c: (b, 0, 0)),
                   pl.BlockSpec((1, SUBLANES, LANES), lambda b, c: (b, 0, 0))],
        out_shape=[jax.ShapeDtypeStruct((B, S, W), bf16),
                   jax.ShapeDtypeStruct((B, W, W), f32),
                   jax.ShapeDtypeStruct((B, W, LANES), f32),
                   jax.ShapeDtypeStruct((B, SUBLANES, LANES), f32)],
        scratch_shapes=[pltpu.VMEM((W, W), f32), pltpu.VMEM((W, LANES), f32),
                        pltpu.VMEM((1, LANES), f32), pltpu.VMEM((SUBLANES, LANES), f32),
                        pltpu.VMEM((SUBLANES, W), f32)],
        compiler_params=_params(("arbitrary", "arbitrary")),
        name="mlstm_prompt",
    )(zm, cw, cb, wq, wk, wkt, wv, bi, bfg)


MOBA_VH = C_DH + 16


def _max_rows(x):
    r = x.shape[0]
    while r > SUBLANES:
        r //= 2
        x = jnp.maximum(x[:r], x[r:])
    return jnp.max(x, axis=0, keepdims=True)


def _moba_prompt_kernel(tab_ref, q_ref, k_ref, v_ref, o_ref,
                        k16, vt16, kmean, b_own, b_prev, sel_ref, qm_ref, acc_ref, s_ref, p_ref):
    S = k_ref.shape[1]
    nb = S // C_BLOCK
    T = C_BLOCK
    H = C_HEADS
    i = pl.program_id(1)
    key_i = lax.broadcasted_iota(jnp.int32, (T, T), 0)
    qry_i = lax.broadcasted_iota(jnp.int32, (T, T), 1)

    @pl.when(i == 0)
    def _():
        kf = k_ref[0]
        ones = jnp.ones((MOBA_VH - C_DH, T), bf16)
        for j in range(nb):
            k16[j] = kf[j * T:(j + 1) * T, :].astype(bf16)
            vt = v_ref[0, j * T:(j + 1) * T, :].T.astype(bf16)
            vt16[j] = jnp.concatenate([x for h in range(H) for x in (vt[h * C_DH:(h + 1) * C_DH], ones)], axis=0)
        kmean[...] = jnp.sum(kf.reshape(nb, T, C_WIDTH), axis=1) * (1.0 / T)

    @pl.when((i == 0) & (pl.program_id(0) == 0))
    def _():
        for h in range(H):
            b_own[h] = _rel_bias(qry_i - key_i, tab_ref, h)
            b_prev[h] = _rel_bias(qry_i - key_i + T, tab_ref, h)

    qf = q_ref[0]
    lane_head = lax.broadcasted_iota(jnp.int32, (1, C_WIDTH), 1) // C_DH
    jio = lax.broadcasted_iota(jnp.int32, (nb, T), 0)
    valid = jio < i
    scale = C_DH ** -0.5
    for h in range(H):
        hm = lane_head == h
        qm_ref[h] = (jnp.where(hm, qf, 0.0) * scale).astype(bf16)
        gs = _nt(jnp.where(hm, kmean[...], 0.0), qf, HI)
        sel = jnp.zeros((nb, T), f32)
        for j in range(nb):
            vj = gs[j:j + 1, :]
            beats = jnp.where(gs > vj, 1.0, jnp.where((gs == vj) & (jio < j), 1.0, 0.0))
            cnt = jnp.sum(jnp.where(valid, beats, 0.0), axis=0, keepdims=True)
            sel = jnp.where(jio == j, jnp.where(cnt < C_TOPK, 1.0, 0.0), sel)
        sel_ref[h] = (sel - 1.0) * (-NEG)

    rows = lambda h: slice(h * MOBA_VH, (h + 1) * MOBA_VH)

    kblk = k16[i]
    for h in range(H):
        s_ref[h] = _nt(kblk, qm_ref[h])
    ms = []
    for h in range(H):
        s = jnp.where(key_i <= qry_i, s_ref[h] + b_own[h], NEG)
        m0 = _max_rows(s)
        p_ref[h] = jnp.exp(s - m0).astype(bf16)
        ms.append(m0)
    for h in range(H):
        acc_ref[rows(h), :] = _dot(vt16[i, rows(h), :], p_ref[h])

    def past_block(j, bias_of, ms):
        kblk = k16[j]
        for h in range(H):
            s_ref[h] = _nt(kblk, qm_ref[h])
        ms_new, alphas = [], []
        for h in range(H):
            sj = s_ref[h] + (bias_of(h) + sel_ref[h, pl.ds(j, 1), :])
            m_new = jnp.maximum(ms[h], _max_rows(sj))
            alphas.append(jnp.exp(ms[h] - m_new))
            p_ref[h] = jnp.exp(sj - m_new).astype(bf16)
            ms_new.append(m_new)
        for h in range(H):
            acc_ref[rows(h), :] = alphas[h] * acc_ref[rows(h), :] + _dot(vt16[j, rows(h), :], p_ref[h])
        return tuple(ms_new)

    ms = tuple(ms)
    ms = lax.cond(i >= 1, lambda c: past_block(i - 1, lambda h: b_prev[h], c), lambda c: c, ms)
    lax.fori_loop(0, jnp.maximum(i - 1, 0),
                  lambda j, c: past_block(j, lambda h: tab_ref[h * REL_BUCKETS + REL_BUCKETS - 1], c), ms)
    res = jnp.concatenate(
        [acc_ref[h * MOBA_VH:h * MOBA_VH + C_DH, :] / acc_ref[h * MOBA_VH + C_DH:h * MOBA_VH + C_DH + 1, :]
         for h in range(H)], axis=0)
    o_ref[0] = res.T.astype(o_ref.dtype)


def moba_prompt(q, k, v, tab):
    B, S, W = q.shape
    nb = S // C_BLOCK
    T = C_BLOCK
    return pl.pallas_call(
        _moba_prompt_kernel,
        grid=(B, nb),
        in_specs=[pl.BlockSpec(memory_space=pltpu.SMEM),
                  pl.BlockSpec((1, T, W), lambda b, i: (b, i, 0)),
                  pl.BlockSpec((1, S, W), lambda b, i: (b, 0, 0)),
                  pl.BlockSpec((1, S, W), lambda b, i: (b, 0, 0))],
        out_specs=pl.BlockSpec((1, T, W), lambda b, i: (b, i, 0)),
        out_shape=jax.ShapeDtypeStruct((B, S, W), bf16),
        scratch_shapes=[pltpu.VMEM((nb, T, W), bf16), pltpu.VMEM((nb, C_HEADS * MOBA_VH, T), bf16),
                        pltpu.VMEM((nb, W), f32),
                        pltpu.VMEM((C_HEADS, T, T), f32), pltpu.VMEM((C_HEADS, T, T), f32),
                        pltpu.VMEM((C_HEADS, nb, T), f32), pltpu.VMEM((C_HEADS, T, W), bf16),
                        pltpu.VMEM((C_HEADS * MOBA_VH, T), f32),
                        pltpu.VMEM((C_HEADS, T, T), f32), pltpu.VMEM((C_HEADS, T, T), bf16)],
        compiler_params=_params(("arbitrary", "arbitrary")),
        name="moba_prompt",
    )(tab, q, k, v)


def _layer_out_kernel(x_ref, ya_ref, ym_ref, yc_ref, sg_ref, g1_ref, sh2_ref, sc2_ref, g2_ref,
                      wa_ref, wm_ref, wc_ref, wo_ref, l1g_ref, l1b_ref, l2g_ref, l2b_ref,
                      wg_ref, wu_ref, wd_ref, o_ref, x1_ref, h2_ref, f_ref, *, alpha):
    kf = pl.program_id(2)
    d = D_MODEL

    @pl.when(kf == 0)
    def _():
        pa = _dot(ya_ref[0], wa_ref[...])
        pm = _dot(ym_ref[0], wm_ref[...])
        pc = _dot(yc_ref[0], wc_ref[...])
        pre = (sg_ref[0, :, :d].astype(f32) * pa + sg_ref[0, :, d:2 * d].astype(f32) * pm
               + sg_ref[0, :, 2 * d:].astype(f32) * pc)
        mix = _dot(pre.astype(wo_ref.dtype), wo_ref[...])
        x1 = _layer_norm(alpha * x_ref[0] + g1_ref[0] * mix, l1g_ref[...], l1b_ref[...])
        x1_ref[...] = x1
        h2_ref[...] = (x1 * (1.0 + sc2_ref[0]) + sh2_ref[0]).astype(h2_ref.dtype)
        f_ref[...] = jnp.zeros_like(f_ref)

    h2 = h2_ref[...]
    t = _silu(_dot(h2, wg_ref[...])) * _dot(h2, wu_ref[...])
    f_ref[...] += _dot(t.astype(wd_ref.dtype), wd_ref[...])

    @pl.when(kf == pl.num_programs(2) - 1)
    def _():
        o_ref[0] = _layer_norm(alpha * x1_ref[...] + g2_ref[0] * f_ref[...], l2g_ref[...], l2b_ref[...])


def layer_out(x, ya, ym, yc, sg, mods, wa, wm, wc, wo, l1g, l1b, l2g, l2b, wg, wu, wd, tm, tf, alpha):
    B, S, d = x.shape
    dff = wg.shape[1]
    tok = lambda a: pl.BlockSpec((1, tm, a.shape[2]), lambda b, i, k: (b, i, 0))
    full = lambda a: pl.BlockSpec(a.shape, lambda b, i, k: (0,) * a.ndim, pipeline_mode=pl.Buffered(1))
    return pl.pallas_call(
        functools.partial(_layer_out_kernel, alpha=alpha),
        grid=(B, S // tm, dff // tf),
        in_specs=[tok(x), tok(ya), tok(ym), tok(yc), tok(sg),
                  _mod_spec(mods, tm, 2), _mod_spec(mods, tm, 3), _mod_spec(mods, tm, 4), _mod_spec(mods, tm, 5),
                  full(wa), full(wm), full(wc), full(wo), full(l1g), full(l1b), full(l2g), full(l2b),
                  pl.BlockSpec((d, tf), lambda b, i, k: (0, k)),
                  pl.BlockSpec((d, tf), lambda b, i, k: (0, k)),
                  pl.BlockSpec((tf, d), lambda b, i, k: (k, 0))],
        out_specs=pl.BlockSpec((1, tm, d), lambda b, i, k: (b, i, 0)),
        out_shape=jax.ShapeDtypeStruct((B, S, d), f32),
        scratch_shapes=[pltpu.VMEM((tm, d), f32), pltpu.VMEM((tm, d), bf16), pltpu.VMEM((tm, d), f32)],
        compiler_params=_params(("arbitrary", "arbitrary", "arbitrary")),
        name="layer_out",
    )(x, ya, ym, yc, sg, mods, mods, mods, mods, wa, wm, wc, wo, l1g, l1b, l2g, l2b, wg, wu, wd)


def _absorb_kernel(qh_ref, wabs_ref, o_ref):
    for h in range(A_HEADS):
        blk = slice(h * HEAD_BLK, (h + 1) * HEAD_BLK)
        o_ref[:, blk] = _dot(qh_ref[:, blk], wabs_ref[h], HI)


def mla_absorb(qh, wabs):
    return pl.pallas_call(_absorb_kernel, out_shape=jax.ShapeDtypeStruct(qh.shape, f32),
                          compiler_params=_params(None), name="mla_absorb")(qh, wabs)


def _paged_kernel(pt_ref, qa_ref, qb_ref, latn_ref, krn_ref, qcol_ref, lat_hbm, krt_hbm, kt_hbm,
                  olat_ref, idx_ref, lat_buf, kr_buf, kt_buf, sems, m_s, l_s, acc_s, g_s,
                  *, layer, pages, cps):
    g = pl.program_id(0)
    c = g % cps
    slot = g % 2
    ppb = C_BLOCK // PAGE_SIZE
    blocks = pages // ppb

    def copies(step, sl):
        b = step // cps
        first = (step % cps) * pages
        out = []
        for j in range(pages):
            pg = pt_ref[b, first + j]
            out.append(pltpu.make_async_copy(lat_hbm.at[layer, pg], lat_buf.at[sl, j], sems.at[sl, 0]))
            out.append(pltpu.make_async_copy(krt_hbm.at[layer, pg], kr_buf.at[sl, j], sems.at[sl, 1]))
            out.append(pltpu.make_async_copy(kt_hbm.at[layer, pg], kt_buf.at[sl, j], sems.at[sl, 2]))
        return out

    @pl.when(g == 0)
    def _():
        for cp in copies(0, 0):
            cp.start()

    @pl.when(g + 1 < pl.num_programs(0))
    def _():
        for cp in copies(g + 1, 1 - slot):
            cp.start()

    qa = qa_ref[0]
    qr = qb_ref[0][:, A_NOPE:A_NOPE + A_ROPE]

    @pl.when(c == 0)
    def _():
        latn = latn_ref[0]
        s0 = jnp.sum(qa * latn, -1, keepdims=True) + jnp.sum(qr * krn_ref[0], -1, keepdims=True)
        m_s[...] = s0
        l_s[...] = jnp.ones_like(l_s)
        acc_s[...] = jnp.broadcast_to(latn, acc_s.shape)

    for cp in copies(g, slot):
        cp.wait()

    lat = lat_buf[slot].reshape(pages * PAGE_SIZE, A_KV_LORA).astype(bf16)
    krt = jnp.concatenate([kr_buf[slot, j] for j in range(pages)], axis=1).astype(bf16)
    s = _nt(qa.astype(bf16), lat) + _dot(qr.astype(bf16), krt)
    m_old = m_s[...]
    m_new = jnp.maximum(m_old, jnp.max(s, -1, keepdims=True))
    alpha = jnp.exp(m_old - m_new)
    p = jnp.exp(s - m_new)
    l_s[...] = alpha * l_s[...] + jnp.sum(p, -1, keepdims=True)
    acc_s[...] = alpha * acc_s[...] + _dot(p.astype(bf16), lat)
    m_s[...] = m_new

    qcb = jnp.broadcast_to(qcol_ref[0], (C_WIDTH, PAGE_SIZE))
    for jj in range(blocks):
        ks = kt_buf[slot, jj * ppb]
        for t in range(1, ppb):
            ks = ks + kt_buf[slot, jj * ppb + t]
        w = ks * qcb
        for h in range(C_HEADS):
            row = jnp.sum(w[h * C_DH:(h + 1) * C_DH, :], axis=0, keepdims=True)
            g_s[pl.ds(h * (cps * blocks) + c * blocks + jj, 1), :] = row

    @pl.when(c == cps - 1)
    def _():
        olat_ref[0] = acc_s[...] / l_s[...]
        nbp = cps * blocks
        gsum = _nt(jnp.ones((SUBLANES, PAGE_SIZE), f32), g_s[...], HI) * (1.0 / C_BLOCK)
        row = lax.broadcasted_iota(jnp.int32, gsum.shape, 0)
        lane_i = lax.broadcasted_iota(jnp.int32, gsum.shape, 1)
        lane = lane_i.astype(f32)
        gm = jnp.where(lane_i // nbp == row, gsum, -jnp.inf)
        out_lane = lax.broadcasted_iota(jnp.int32, (SUBLANES, LANES), 1)
        out = jnp.zeros((SUBLANES, LANES), f32)
        for r in range(C_TOPK):
            mx = jnp.max(gm, -1, keepdims=True)
            ix = jnp.min(jnp.where(gm == mx, lane, float(gm.shape[1])), -1, keepdims=True)
            out = jnp.where(out_lane == r, ix, out)
            gm = jnp.where(lane == ix, -jnp.inf, gm)
        head_off = (lax.broadcasted_iota(jnp.int32, (SUBLANES, LANES), 0) * nbp).astype(f32)
        idx_ref[0] = (out - head_off).astype(jnp.int32)


def sample_paged(page_table, qa, qb, latn, krn, qcol, cache_lat, cache_krt, cache_kt, layer, pages):
    DB, n_pages = page_table.shape
    cps = n_pages // pages
    nbp = n_pages * PAGE_SIZE // C_BLOCK
    seq = lambda a: pl.BlockSpec((1,) + a.shape[1:], lambda g, pt: (g // cps, 0, 0))
    hbm = pl.BlockSpec(memory_space=pl.ANY)
    grid_spec = pltpu.PrefetchScalarGridSpec(
        num_scalar_prefetch=1,
        grid=(DB * cps,),
        in_specs=[seq(qa), seq(qb), seq(latn), seq(krn), seq(qcol), hbm, hbm, hbm],
        out_specs=[pl.BlockSpec((1, A_HEADS, A_KV_LORA), lambda g, pt: (g // cps, 0, 0)),
                   pl.BlockSpec((1, SUBLANES, LANES), lambda g, pt: (g // cps, 0, 0))],
        scratch_shapes=[pltpu.VMEM((2, pages, PAGE_SIZE, A_KV_LORA), f32),
                        pltpu.VMEM((2, pages, A_ROPE, PAGE_SIZE), f32),
                        pltpu.VMEM((2, pages, C_WIDTH, PAGE_SIZE), f32),
                        pltpu.SemaphoreType.DMA((2, 3)),
                        pltpu.VMEM((A_HEADS, 1), f32), pltpu.VMEM((A_HEADS, 1), f32),
                        pltpu.VMEM((A_HEADS, A_KV_LORA), f32),
                        pltpu.VMEM((C_HEADS * nbp, PAGE_SIZE), f32)],
    )
    return pl.pallas_call(
        functools.partial(_paged_kernel, layer=layer, pages=pages, cps=cps),
        grid_spec=grid_spec,
        out_shape=[jax.ShapeDtypeStruct((DB, A_HEADS, A_KV_LORA), f32),
                   jax.ShapeDtypeStruct((DB, SUBLANES, LANES), jnp.int32)],
        compiler_params=_params(("arbitrary",)),
        name="sample_paged",
    )(page_table, qa, qb, latn, krn, qcol, cache_lat, cache_krt, cache_kt)


def _value_up_kernel(o_ref, w_ref, y_ref):
    y_ref[...] = _dot(o_ref[...], w_ref[...], HI).astype(y_ref.dtype)


def mla_value_up(olat, wuv_bd):
    return pl.pallas_call(_value_up_kernel,
                          out_shape=jax.ShapeDtypeStruct((olat.shape[0], wuv_bd.shape[1]), bf16),
                          compiler_params=_params(None), name="mla_value_up")(olat, wuv_bd)


def _mlstm_step_kernel(zm_ref, cs_ref, cw_ref, cb_ref, wqt_ref, wkt_ref, wvt_ref, bi_ref, bf_ref,
                       c0_ref, n0_ref, m0_ref, h_ref, c1_ref, n1_ref, m1_ref, qt_s, kt_s):
    W = M_WIDTH
    zm = zm_ref[...]
    u = zm[:, :W]
    acc = u * cw_ref[M_CONV - 1:M_CONV, :]
    for j in range(M_CONV - 1):
        acc = acc + cs_ref[j] * cw_ref[j:j + 1, :]
    uc = _silu(acc + cb_ref[...])
    qt_s[...] = _nt(wqt_ref[...], uc, HI)
    kt_s[...] = _nt(wkt_ref[...], uc, HI)
    vt = _nt(wvt_ref[...], u, HI)
    gate_t = jax.nn.sigmoid(zm[:, W:2 * W]).T
    i_t = (zm[:, 2 * W:2 * W + LANES] + bi_ref[...]).T[:SUBLANES, :]
    f_t = _log_sigmoid(zm[:, 2 * W + LANES:] + bf_ref[...]).T[:SUBLANES, :]
    outs = []
    for h in range(M_HEADS):
        rows = slice(h * M_DH, (h + 1) * M_DH)
        i_h = i_t[h:h + 1, :]
        a = f_t[h:h + 1, :] + m0_ref[h:h + 1, :]
        mt = jnp.maximum(a, i_h)
        w_old = jnp.exp(a - mt)
        w_tok = jnp.exp(i_h - mt)
        qh = qt_s[rows, :]
        kh = kt_s[rows, :]
        vh = vt[rows, :]
        s = jnp.sum(qh * kh, axis=0, keepdims=True) * w_tok
        qn = jnp.sum(qh * n0_ref[h], axis=0, keepdims=True)

        def body(d, qc, h=h, w_old=w_old, w_tok=w_tok, vh=vh):
            c0 = c0_ref[h, d]
            k_row = kt_s[pl.ds(h * M_DH + d, 1), :]
            q_row = qt_s[pl.ds(h * M_DH + d, 1), :]
            c1_ref[h, d] = w_old * c0 + (w_tok * k_row) * vh
            return qc + q_row * c0

        qc = lax.fori_loop(0, M_DH, body, jnp.zeros((M_DH, zm.shape[0]), f32))
        num = w_old * qc + s * vh
        den = w_old * qn + s
        outs.append(num / jnp.maximum(jnp.abs(den), jnp.exp(-mt)) * gate_t[rows, :])
        n1_ref[h] = w_old * n0_ref[h] + w_tok * kh
        m1_ref[h:h + 1, :] = mt
    h_ref[...] = jnp.concatenate(outs, axis=0).T.astype(h_ref.dtype)


def mlstm_sample_step(zm, conv_t, cw, cb, wqt, wkt, wvt, bi, bfg, c0, n0, m0):
    DB = zm.shape[0]
    shapes = [jax.ShapeDtypeStruct((DB, M_WIDTH), bf16), jax.ShapeDtypeStruct(c0.shape, f32),
              jax.ShapeDtypeStruct(n0.shape, f32), jax.ShapeDtypeStruct(m0.shape, f32)]
    return pl.pallas_call(
        _mlstm_step_kernel, out_shape=shapes,
        scratch_shapes=[pltpu.VMEM((M_WIDTH, DB), f32), pltpu.VMEM((M_WIDTH, DB), f32)],
        compiler_params=_params(None), name="mlstm_sample_step",
    )(zm, conv_t, cw, cb, wqt, wkt, wvt, bi, bfg, c0, n0, m0)


def _moba_decode_kernel(pt_ref, idx_ref, tab_ref, q_ref, kn_ref, vn_ref, kt_hbm, vt_hbm, o_ref,
                        k_buf, v_buf, sems, *, layer, n_sel, past):
    b = pl.program_id(0)
    slot = b % 2
    ppb = C_BLOCK // PAGE_SIZE
    scale = C_DH ** -0.5

    def copies(seq, sl):
        out = []
        for h in range(C_HEADS):
            for j in range(n_sel):
                blk = idx_ref[(seq * C_HEADS + h) * C_TOPK + j // ppb]
                pg = pt_ref[seq, blk * ppb + j % ppb]
                rows = pl.ds(h * C_DH, C_DH)
                out.append(pltpu.make_async_copy(kt_hbm.at[layer, pg, rows], k_buf.at[sl, h * n_sel + j], sems.at[sl, 0]))
                out.append(pltpu.make_async_copy(vt_hbm.at[layer, pg, rows], v_buf.at[sl, h * n_sel + j], sems.at[sl, 1]))
        return out

    @pl.when(b == 0)
    def _():
        for cp in copies(0, 0):
            cp.start()

    @pl.when(b + 1 < pl.num_programs(0))
    def _():
        for cp in copies(b + 1, 1 - slot):
            cp.start()

    for cp in copies(b, slot):
        cp.wait()

    t = lax.broadcasted_iota(jnp.int32, (1, C_BLOCK), 1)
    heads = range(C_HEADS)
    scores = []
    for h in heads:
        kt = jnp.concatenate([k_buf[slot, h * n_sel + j] for j in range(n_sel)], axis=1).astype(bf16)
        q8 = jnp.broadcast_to((q_ref[h] * scale).astype(bf16), (SUBLANES, C_DH))
        scores.append(_dot(q8, kt)[:1, :])
    probs, p_selfs, ls = [], [], []
    for h in heads:
        q = q_ref[h]
        near = _rel_bias(C_BLOCK - t, tab_ref, h)
        far = tab_ref[h * REL_BUCKETS + REL_BUCKETS - 1]
        newest = past // C_BLOCK - 1
        bias = [jnp.where(idx_ref[(b * C_HEADS + h) * C_TOPK + j] == newest, near, far)
                for j in range(n_sel // ppb)]
        s = scores[h] + jnp.concatenate(bias, axis=1)
        s_self = jnp.sum(q * kn_ref[h], -1, keepdims=True) * scale + tab_ref[h * REL_BUCKETS]
        m = jnp.maximum(jnp.max(s, -1, keepdims=True), s_self)
        p = jnp.exp(s - m)
        p_self = jnp.exp(s_self - m)
        ls.append(jnp.sum(p, -1, keepdims=True) + p_self)
        p_selfs.append(p_self)
        probs.append(jnp.broadcast_to(p.astype(bf16), (SUBLANES, p.shape[1])))
    for h in heads:
        vt = jnp.concatenate([v_buf[slot, h * n_sel + j] for j in range(n_sel)], axis=1).astype(bf16)
        pv = _nt(probs[h], vt)[:1, :]
        o_ref[h] = (pv + p_selfs[h] * vn_ref[h]) / ls[h]


def moba_decode(page_table, idx_flat, tab, q, kn, vn, cache_kt, cache_vt, layer):
    DB, n_pages = page_table.shape
    ppb = C_BLOCK // PAGE_SIZE
    n_sel = C_TOPK * ppb
    past = n_pages * PAGE_SIZE
    seq = pl.BlockSpec((None, C_HEADS, 1, C_DH), lambda b, pt, idx: (b, 0, 0, 0))
    hbm = pl.BlockSpec(memory_space=pl.ANY)
    grid_spec = pltpu.PrefetchScalarGridSpec(
        num_scalar_prefetch=2,
        grid=(DB,),
        in_specs=[pl.BlockSpec(memory_space=pltpu.SMEM), seq, seq, seq, hbm, hbm],
        out_specs=seq,
        scratch_shapes=[pltpu.VMEM((2, C_HEADS * n_sel, C_DH, PAGE_SIZE), f32),
                        pltpu.VMEM((2, C_HEADS * n_sel, C_DH, PAGE_SIZE), f32),
                        pltpu.SemaphoreType.DMA((2, 2))],
    )
    return pl.pallas_call(
        functools.partial(_moba_decode_kernel, layer=layer, n_sel=n_sel, past=past),
        grid_spec=grid_spec,
        out_shape=jax.ShapeDtypeStruct((DB, C_HEADS, 1, C_DH), f32),
        compiler_params=_params(("arbitrary",)),
        name="moba_decode",
    )(page_table, idx_flat, tab, q, kn, vn, cache_kt, cache_vt)


def _pack_w_in(w):
    d = w.shape[0]
    z = lambda n: jnp.zeros((d, n), w.dtype)
    o = 0
    parts = {}
    for name, size in (("cq", A_Q_LORA), ("ckv", A_KV_LORA), ("kr", A_ROPE), ("u", M_WIDTH), ("o", M_WIDTH),
                       ("i", M_HEADS), ("f", M_HEADS), ("qkv", 3 * C_WIDTH), ("g", 3 * D_MODEL)):
        parts[name] = w[:, o:o + size]
        o += size
    kr = parts["kr"]
    half = A_ROPE // 2
    pad = LANES - A_NOPE - A_ROPE
    krp = jnp.concatenate([z(A_NOPE), kr, z(pad)], axis=1)
    krs = jnp.concatenate([z(A_NOPE), kr[:, half:], kr[:, :half], z(pad)], axis=1)
    return jnp.concatenate([parts["cq"], parts["ckv"], krp, krs, parts["u"], parts["o"],
                            parts["i"], z(LANES - M_HEADS), parts["f"], z(LANES - M_HEADS),
                            parts["qkv"], parts["g"]], axis=1)


def _pack_w_q(w):
    r = w.shape[0]
    w3 = w.reshape(r, A_HEADS, A_NOPE + A_ROPE)
    half = A_ROPE // 2
    pad = jnp.zeros((r, A_HEADS, LANES - A_NOPE - A_ROPE), w.dtype)
    nope, rope = w3[..., :A_NOPE], w3[..., A_NOPE:]
    w1 = jnp.concatenate([nope, rope, pad], axis=-1)
    w2 = jnp.concatenate([jnp.zeros_like(nope), rope[..., half:], rope[..., :half], pad], axis=-1)
    return w1.reshape(r, -1), w2.reshape(r, -1)


def _rope_tables(pos):
    inv = ROPE_THETA ** (-jnp.arange(0, A_ROPE, 2, dtype=f32) / A_ROPE)
    ang = pos.astype(f32)[:, None] * inv[None, :]
    cos, sin = jnp.cos(ang), jnp.sin(ang)
    n = pos.shape[0]
    pad = jnp.zeros((n, LANES - A_NOPE - A_ROPE), f32)
    ct = jnp.concatenate([jnp.ones((n, A_NOPE), f32), cos, cos, pad], axis=1)
    st = jnp.concatenate([jnp.zeros((n, A_NOPE), f32), -sin, sin, pad], axis=1)
    return ct, st


def _block_diag(w):
    hN, a, b = w.shape
    eye = jnp.eye(hN, dtype=w.dtype)
    return (eye[:, None, :, None] * w[:, :, None, :]).reshape(hN * a, hN * b)


def _lane_pad(v, width=LANES):
    return jnp.pad(v, ((0, 0), (0, width - v.shape[-1])))


def kernel(x_prompt, x_sample, cache_mla_latent, cache_mla_krope, cache_moba_k, cache_moba_v, state_mlstm_C, state_mlstm_n, state_mlstm_m, state_mlstm_conv, page_table, c_prompt, c_sample, rel_table, w_ada, b_ada, w_in, g_q_norm, w_q_up, g_kv_norm, w_kv_up, conv_w, conv_b, w_mq, w_mk, w_mv, b_i, b_f, w_br_a, w_br_m, w_br_c, w_out, ln1_g, ln1_b, w_ff_gate, w_ff_up, w_ff_down, ln2_g, ln2_b):
    B, S, d = x_prompt.shape
    DB, T, _ = x_sample.shape
    depth = w_in.shape[0]
    n_pool = cache_mla_latent.shape[1]
    n_pages = page_table.shape[1]
    past = n_pages * PAGE_SIZE
    assert T == 1 and past % C_BLOCK == 0 and past // C_BLOCK >= C_TOPK and S % C_BLOCK == 0
    assert S >= M_CONV - 1 and C_BLOCK + 1 >= REL_THRESH[-1]
    alpha = (2 * depth) ** 0.25
    dff = w_ff_gate.shape[2]

    mods_all = ada_mods(jnp.concatenate([c_prompt, c_sample], axis=0), w_ada, b_ada)
    ct_p, st_p = _rope_tables(jnp.arange(S, dtype=jnp.int32))
    ct_s, st_s = _rope_tables(jnp.full((DB,), past, jnp.int32))
    tab = rel_table.T.reshape(-1)
    cache_kt = jnp.transpose(cache_moba_k, (0, 1, 3, 4, 2)).reshape(depth, n_pool, C_WIDTH, PAGE_SIZE)
    cache_vt = jnp.transpose(cache_moba_v, (0, 1, 3, 4, 2)).reshape(depth, n_pool, C_WIDTH, PAGE_SIZE)
    cache_krt = jnp.transpose(cache_mla_krope, (0, 1, 3, 2))
    xs = x_sample.reshape(1, DB, d)
    xp = x_prompt

    new_p = [[] for _ in range(8)]
    new_s = [[] for _ in range(8)]
    for l in range(depth):
        w_pack = _pack_w_in(w_in[l]).astype(bf16)
        wq1, wq2 = _pack_w_q(w_q_up[l])
        w_uk = w_kv_up[l][..., :A_NOPE]
        w_uv = w_kv_up[l][..., A_NOPE:]
        wuk = jnp.concatenate([w_uk, jnp.zeros_like(w_uk)], axis=-1).reshape(A_KV_LORA, -1)
        wuv = w_uv.reshape(A_KV_LORA, -1)
        gq = g_q_norm[l][None, :]
        gkv = g_kv_norm[l][None, :]
        wq_bd = _block_diag(w_mq[l])
        wk_bd = _block_diag(w_mk[l]) * (M_DH ** -0.5)
        wv_bd = _block_diag(w_mv[l])
        bi = _lane_pad(b_i[l][None, :])
        bfg = _lane_pad(b_f[l][None, :])
        cw = conv_w[l]
        cb = conv_b[l][None, :]
        tail_w = (w_br_a[l].astype(bf16), w_br_m[l].astype(bf16), w_br_c[l].astype(bf16), w_out[l].astype(bf16),
                  ln1_g[l][None, :], ln1_b[l][None, :], ln2_g[l][None, :], ln2_b[l][None, :],
                  w_ff_gate[l].astype(bf16), w_ff_up[l].astype(bf16), w_ff_down[l].astype(bf16))

        mods_p = mods_all[l, :B].reshape(B, 1, 6 * d)
        za, zm, qc, kc, vc, sg = in_proj(xp, mods_p, w_pack, tm=512)
        qh, kh, vh, lat, krope = mla_prep(za, ct_p, st_p, gq, gkv, wq1.astype(bf16), wq2.astype(bf16),
                                          wuk.astype(bf16), wuv.astype(bf16), tm=512, qk_dtype=bf16)
        ya = mla_attn(qh, kh, vh, tq=512)
        ym, c_st, n_st, m_st = mlstm_prompt(zm, cw, cb, wq_bd.astype(bf16), wk_bd.astype(bf16),
                                            wk_bd.T.astype(bf16), wv_bd.astype(bf16), bi, bfg, L=256)
        yc = moba_prompt(qc, kc, vc, tab)
        xp_new = layer_out(xp, ya, ym, yc, sg, mods_p, *tail_w, tm=512, tf=dff // 2, alpha=alpha)
        C1 = jnp.stack([c_st[:, h * M_DH:(h + 1) * M_DH, h * M_DH:(h + 1) * M_DH] for h in range(M_HEADS)], axis=1)
        n1 = jnp.stack([n_st[:, h * M_DH:(h + 1) * M_DH, h] for h in range(M_HEADS)], axis=1)
        m1 = m_st[:, :M_HEADS, 0]
        u_p = zm[:, :, :M_WIDTH]
        conv_p = u_p[:, S - (M_CONV - 1):]
        for lst, val in zip(new_p, (lat, krope, kc.reshape(B, S, C_HEADS, C_DH), vc.reshape(B, S, C_HEADS, C_DH),
                                    C1, n1, m1, conv_p)):
            lst.append(val)
        xp = xp_new

        mods_s = mods_all[l, B:].reshape(1, DB, 6 * d)
        za, zm, qc, kc, vc, sg = in_proj(xs, mods_s, w_pack, tm=DB)
        qh, _, _, lat, krope = mla_prep(za, ct_s, st_s, gq, gkv, wq1, wq2, wuk, wuv, tm=DB, qk_dtype=f32)
        wabs = jnp.concatenate([jnp.transpose(w_uk, (1, 2, 0)),
                                jnp.zeros((A_HEADS, HEAD_BLK - A_NOPE, A_KV_LORA), f32)], axis=1)
        qh2 = qh.reshape(DB, A_HEADS * HEAD_BLK)
        qabs = mla_absorb(qh2, wabs)
        olat, idx = sample_paged(page_table, qabs.reshape(DB, A_HEADS, A_KV_LORA), qh2.reshape(DB, A_HEADS, HEAD_BLK),
                                 lat.reshape(DB, 1, A_KV_LORA), krope.reshape(DB, 1, A_ROPE),
                                 qc.reshape(DB, C_WIDTH, 1), cache_mla_latent, cache_krt, cache_kt, l, pages=64)
        wuv_bd = _block_diag(jnp.transpose(w_uv, (1, 0, 2)))
        ya = mla_value_up(olat.reshape(DB, A_HEADS * A_KV_LORA), wuv_bd)

        zm2 = zm.reshape(DB, ZM_W)
        conv_state = state_mlstm_conv[l]
        ym, C1t, n1t, m1t = mlstm_sample_step(
            zm2, jnp.transpose(conv_state, (1, 0, 2)), cw, cb, wq_bd.T, wk_bd.T, wv_bd.T, bi, bfg,
            jnp.transpose(state_mlstm_C[l], (1, 2, 3, 0)), jnp.transpose(state_mlstm_n[l], (1, 2, 0)),
            jnp.transpose(state_mlstm_m[l], (1, 0)))
        C1 = jnp.transpose(C1t, (3, 0, 1, 2))
        n1 = jnp.transpose(n1t, (2, 0, 1))
        m1 = jnp.transpose(m1t, (1, 0))
        u_s = zm2[:, :M_WIDTH]
        conv_s = jnp.concatenate([conv_state[:, 1:], u_s[:, None, :]], axis=1)

        idx_flat = idx[:, :C_HEADS, :C_TOPK].reshape(-1)
        per_head = lambda a: a.reshape(DB, C_HEADS, 1, C_DH)
        yc = moba_decode(page_table, idx_flat, tab, per_head(qc), per_head(kc), per_head(vc), cache_kt, cache_vt, l)
        xs_new = layer_out(xs, ya.reshape(1, DB, -1), ym.reshape(1, DB, -1), yc.reshape(1, DB, -1).astype(bf16),
                           sg, mods_s, *tail_w, tm=DB, tf=dff // 2, alpha=alpha)
        for lst, val in zip(new_s, (lat.reshape(DB, 1, A_KV_LORA), krope.reshape(DB, 1, A_ROPE),
                                    kc.reshape(DB, 1, C_HEADS, C_DH), vc.reshape(DB, 1, C_HEADS, C_DH),
                                    C1.reshape(DB, M_HEADS, M_DH, M_DH), n1.reshape(DB, M_HEADS, M_DH),
                                    m1.reshape(DB, M_HEADS), conv_s)):
            lst.append(val)
        xs = xs_new

    outs_p = [jnp.stack(a) for a in new_p]
    outs_s = [jnp.stack(a) for a in new_s]
    return (xp, xs.reshape(DB, T, d), *outs_p, *outs_s)
```

```python
import functools
import math

import jax
import jax.numpy as jnp
from jax import lax
from jax.experimental import pallas as pl
from jax.experimental.pallas import tpu as pltpu

f32 = jnp.float32
bf16 = jnp.bfloat16
HI = lax.Precision.HIGHEST

D_MODEL = 1024
PAGE_SIZE = 128
A_HEADS = 8
A_NOPE = 64
A_ROPE = 32
A_VDIM = 64
A_Q_LORA = 256
A_KV_LORA = 128
ROPE_THETA = 10000.0
M_HEADS = 4
M_DH = 64
M_WIDTH = M_HEADS * M_DH
M_CONV = 4
C_HEADS = 4
C_DH = 64
C_WIDTH = C_HEADS * C_DH
C_BLOCK = 256
C_TOPK = 3
REL_BUCKETS = 32
REL_MAX_DIST = 128
LN_EPS = 1e-5
RMS_EPS = 1e-6
LANES = 128
SUBLANES = 8
VMEM_LIMIT = 56 * 1024 * 1024
NEG = -1e30

ZA_W = A_Q_LORA + A_KV_LORA + 2 * LANES
ZM_W = 2 * M_WIDTH + 2 * LANES
ZC_W = 3 * C_WIDTH
ZG_W = 3 * D_MODEL
PACK_W = ZA_W + ZM_W + ZC_W + ZG_W
HEAD_BLK = LANES


def _rel_thresholds():
    exact = REL_BUCKETS // 2
    out = []
    for k in range(1, REL_BUCKETS - exact):
        n = exact
        while int(math.log(n / exact) / math.log(REL_MAX_DIST / exact) * (REL_BUCKETS - exact)) < k:
            n += 1
        out.append(n)
    return tuple(out)


REL_THRESH = _rel_thresholds()


def _nt(a, b, precision=None):
    return lax.dot_general(a, b, (((1,), (1,)), ((), ())), precision=precision, preferred_element_type=f32)


def _tn(a, b):
    return lax.dot_general(a, b, (((0,), (0,)), ((), ())), preferred_element_type=f32)


def _dot(a, b, precision=None):
    return jnp.dot(a, b, precision=precision, preferred_element_type=f32)


def _silu(x):
    return x * jax.nn.sigmoid(x)


def _layer_norm(x, g, b):
    mu = jnp.mean(x, -1, keepdims=True)
    xc = x - mu
    var = jnp.mean(xc * xc, -1, keepdims=True)
    return xc * lax.rsqrt(var + LN_EPS) * g + b


def _rms_norm(x, g):
    return x * lax.rsqrt(jnp.mean(x * x, -1, keepdims=True) + RMS_EPS) * g


def _rel_bias(dist, tab_ref, h):
    n = jnp.maximum(dist, 0)
    exact = REL_BUCKETS // 2
    large = jnp.full(n.shape, exact, jnp.int32)
    for t in REL_THRESH:
        large = large + (n >= t).astype(jnp.int32)
    bucket = jnp.where(n < exact, n, large)
    val = jnp.full(n.shape, tab_ref[h * REL_BUCKETS + REL_BUCKETS - 1], f32)
    for b in range(REL_BUCKETS - 1):
        val = jnp.where(bucket == b, tab_ref[h * REL_BUCKETS + b], val)
    return val


def _params(sem):
    return pltpu.CompilerParams(dimension_semantics=sem, vmem_limit_bytes=VMEM_LIMIT)


def _ada_kernel(c_ref, w_ref, b_ref, o_ref):
    c = c_ref[...]
    o_ref[...] = _dot(_silu(c), w_ref[...], HI) + b_ref[...]


def ada_mods(c_all, w_ada, b_ada):
    depth, d, n = w_ada.shape
    rows = c_all.shape[0]
    tn = n // 4
    return pl.pallas_call(
        _ada_kernel,
        grid=(depth, n // tn),
        in_specs=[
            pl.BlockSpec((rows, d), lambda l, j: (0, 0)),
            pl.BlockSpec((None, d, tn), lambda l, j: (l, 0, j)),
            pl.BlockSpec((None, 1, tn), lambda l, j: (l, 0, j)),
        ],
        out_specs=pl.BlockSpec((None, rows, tn), lambda l, j: (l, 0, j)),
        out_shape=jax.ShapeDtypeStruct((depth, rows, n), f32),
        compiler_params=_params(("arbitrary", "arbitrary")),
        name="ada_mods",
    )(c_all, w_ada, b_ada.reshape(depth, 1, n))


def _mod_spec(mods, tm, col):
    per_row = mods.shape[1] != 1
    rows = tm if per_row else 1
    return pl.BlockSpec((1, rows, D_MODEL), lambda b, i, *_: (b, i if per_row else 0, col))


def _in_proj_kernel(x_ref, sh_ref, sc_ref, w_ref, za_ref, zm_ref, q_ref, k_ref, v_ref, sg_ref):
    h = (x_ref[0] * (1.0 + sc_ref[0]) + sh_ref[0]).astype(w_ref.dtype)

    def mm(c0, c1):
        return _dot(h, w_ref[:, c0:c1])

    c = 0
    za_ref[0] = mm(c, c + ZA_W)
    c += ZA_W
    zm_ref[0] = mm(c, c + ZM_W)
    c += ZM_W
    q_ref[0] = mm(c, c + C_WIDTH)
    k_ref[0] = mm(c + C_WIDTH, c + 2 * C_WIDTH)
    v_ref[0] = mm(c + 2 * C_WIDTH, c + 3 * C_WIDTH)
    c += ZC_W
    sg_ref[0] = jax.nn.sigmoid(mm(c, c + ZG_W)).astype(sg_ref.dtype)


def in_proj(x, mods, w_pack, tm):
    B, S, d = x.shape
    widths = (ZA_W, ZM_W, C_WIDTH, C_WIDTH, C_WIDTH, ZG_W)
    dts = (f32, f32, f32, f32, f32, bf16)
    return pl.pallas_call(
        _in_proj_kernel,
        grid=(B, S // tm),
        in_specs=[
            pl.BlockSpec((1, tm, d), lambda b, i: (b, i, 0)),
            _mod_spec(mods, tm, 0),
            _mod_spec(mods, tm, 1),
            pl.BlockSpec((d, PACK_W), lambda b, i: (0, 0)),
        ],
        out_specs=[pl.BlockSpec((1, tm, w), lambda b, i: (b, i, 0)) for w in widths],
        out_shape=[jax.ShapeDtypeStruct((B, S, w), t) for w, t in zip(widths, dts)],
        compiler_params=_params(("arbitrary", "arbitrary")),
        name="in_proj",
    )(x, mods, mods, w_pack)


def _mla_prep_kernel(za_ref, ct_ref, st_ref, gq_ref, gkv_ref, wq1_ref, wq2_ref, wuk_ref, wuv_ref,
                     qh_ref, kh_ref, vh_ref, lat_ref, kr_ref, *, prec):
    za = za_ref[0]
    cq = za[:, :A_Q_LORA]
    ckv = za[:, A_Q_LORA:A_Q_LORA + A_KV_LORA]
    krp = za[:, A_Q_LORA + A_KV_LORA:A_Q_LORA + A_KV_LORA + LANES]
    krs = za[:, A_Q_LORA + A_KV_LORA + LANES:]
    ct = ct_ref[...]
    st = st_ref[...]
    scale = (A_NOPE + A_ROPE) ** -0.5
    cqn = _rms_norm(cq, gq_ref[...]).astype(wq1_ref.dtype)
    q1 = _dot(cqn, wq1_ref[...], prec)
    q2 = _dot(cqn, wq2_ref[...], prec)
    lat = _rms_norm(ckv, gkv_ref[...])
    lat_ref[0] = lat
    kro = krp * ct + krs * st
    kr_ref[0] = kro[:, A_NOPE:A_NOPE + A_ROPE]
    latc = lat.astype(wuk_ref.dtype)
    kn = _dot(latc, wuk_ref[...], prec)
    for h in range(A_HEADS):
        blk = slice(h * HEAD_BLK, (h + 1) * HEAD_BLK)
        qh_ref[0, :, blk] = ((q1[:, blk] * ct + q2[:, blk] * st) * scale).astype(qh_ref.dtype)
        kh_ref[0, :, blk] = (kn[:, blk] + kro).astype(kh_ref.dtype)
    vh_ref[0] = _dot(latc, wuv_ref[...], prec).astype(vh_ref.dtype)


def mla_prep(za, ctab, stab, gq, gkv, wq1, wq2, wuk, wuv, tm, qk_dtype):
    B, S, _ = za.shape
    full = lambda a: pl.BlockSpec(a.shape, lambda b, i: (0,) * a.ndim)
    tok = lambda w: pl.BlockSpec((1, tm, w), lambda b, i: (b, i, 0))
    widths = (A_HEADS * HEAD_BLK, A_HEADS * HEAD_BLK, A_HEADS * A_VDIM, A_KV_LORA, A_ROPE)
    dts = (qk_dtype, qk_dtype, qk_dtype, f32, f32)
    prec = HI if qk_dtype == f32 else None
    return pl.pallas_call(
        functools.partial(_mla_prep_kernel, prec=prec),
        grid=(B, S // tm),
        in_specs=[tok(ZA_W),
                  pl.BlockSpec((tm, LANES), lambda b, i: (i, 0)),
                  pl.BlockSpec((tm, LANES), lambda b, i: (i, 0)),
                  full(gq), full(gkv), full(wq1), full(wq2), full(wuk), full(wuv)],
        out_specs=[tok(w) for w in widths],
        out_shape=[jax.ShapeDtypeStruct((B, S, w), t) for w, t in zip(widths, dts)],
        compiler_params=_params(("arbitrary", "arbitrary")),
        name="mla_prep",
    )(za, ctab, stab, gq, gkv, wq1, wq2, wuk, wuv)


def _mla_attn_kernel(q_ref, k_ref, v_ref, o_ref, *, tq):
    S = q_ref.shape[1]
    lane = lax.broadcasted_iota(jnp.int32, (tq, 2 * A_VDIM), 1)
    for t in range(S // tq):
        kv_len = (t + 1) * tq
        row = t * tq + lax.broadcasted_iota(jnp.int32, (tq, kv_len), 0)
        col = lax.broadcasted_iota(jnp.int32, (tq, kv_len), 1)
        causal = col <= row
        vpair = v_ref[0, :kv_len, :]
        outs = []
        for hh in range(2):
            blk = slice(hh * HEAD_BLK, (hh + 1) * HEAD_BLK)
            s = _nt(q_ref[0, t * tq:(t + 1) * tq, blk], k_ref[0, :kv_len, blk])
            s = jnp.where(causal, s, -jnp.inf)
            m = jnp.max(s, -1, keepdims=True)
            p = jnp.exp(s - m)
            l = jnp.sum(p, -1, keepdims=True)
            outs.append(_dot(p.astype(vpair.dtype), vpair) / l)
        o_ref[0, t * tq:(t + 1) * tq, :] = jnp.where(lane < A_VDIM, outs[0], outs[1]).astype(o_ref.dtype)


def mla_attn(qh, kh, vh, tq):
    B, S, _ = qh.shape
    pairs = A_HEADS // 2
    return pl.pallas_call(
        functools.partial(_mla_attn_kernel, tq=tq),
        grid=(B, pairs),
        in_specs=[pl.BlockSpec((1, S, 2 * HEAD_BLK), lambda b, p: (b, 0, p)),
                  pl.BlockSpec((1, S, 2 * HEAD_BLK), lambda b, p: (b, 0, p)),
                  pl.BlockSpec((1, S, 2 * A_VDIM), lambda b, p: (b, 0, p))],
        out_specs=pl.BlockSpec((1, S, 2 * A_VDIM), lambda b, p: (b, 0, p)),
        out_shape=jax.ShapeDtypeStruct((B, S, A_HEADS * A_VDIM), bf16),
        compiler_params=_params(("arbitrary", "arbitrary")),
        name="mla_attn",
    )(qh, kh, vh)


def _log_sigmoid(x):
    return jnp.minimum(x, 0.0) - jnp.log1p(jnp.exp(-jnp.abs(x)))


def _mlstm_kernel(zm_ref, cw_ref, cb_ref, wq_ref, wk_ref, wkt_ref, wv_ref, bi_ref, bf_ref,
                  ym_ref, c_out_ref, n_out_ref, m_out_ref,
                  c_st, n_st, m_row, m_col, tail):
    L = zm_ref.shape[1]
    W = M_WIDTH
    c = pl.program_id(1)

    @pl.when(c == 0)
    def _():
        c_st[...] = jnp.zeros_like(c_st)
        n_st[...] = jnp.zeros_like(n_st)
        m_row[...] = jnp.zeros_like(m_row)
        m_col[...] = jnp.zeros_like(m_col)
        tail[...] = jnp.zeros_like(tail)

    zm = zm_ref[0]
    u = zm[:, :W]
    o_raw = zm[:, W:2 * W]
    ig = zm[:, 2 * W:2 * W + LANES]
    fg = zm[:, 2 * W + LANES:]

    rows = lax.broadcasted_iota(jnp.int32, (L, W), 0)
    tl = tail[...]
    acc = u * cw_ref[M_CONV - 1:M_CONV, :]
    for j in range(1, M_CONV):
        ru = pltpu.roll(u, j, 0)
        rt = jnp.concatenate([pltpu.roll(tl, j, 0), jnp.zeros((L - SUBLANES, W), f32)], axis=0)
        acc = acc + jnp.where(rows < j, rt, ru) * cw_ref[M_CONV - 1 - j:M_CONV - j, :]
    tail[...] = u[L - SUBLANES:, :]
    uc = _silu(acc + cb_ref[...]).astype(bf16)
    ub = u.astype(bf16)

    q = _dot(uc, wq_ref[...])
    k = _dot(uc, wk_ref[...])
    kt = _nt(wkt_ref[...], uc)
    v = _dot(ub, wv_ref[...])
    qb = q.astype(bf16)
    kb = k.astype(bf16)
    vb = v.astype(bf16)

    i_c = ig + bi_ref[...]
    f_c = _log_sigmoid(fg + bf_ref[...])
    t_i = lax.broadcasted_iota(jnp.int32, (L, L), 0)
    s_i = lax.broadcasted_iota(jnp.int32, (L, L), 1)
    causal = s_i <= t_i
    tri = causal.astype(f32)
    b_c = _dot(tri, f_c, HI)
    b_r = b_c.T[:SUBLANES, :]
    i_r = i_c.T[:SUBLANES, :]
    a_c = b_c + m_row[...]

    inter = _dot(qb, c_st[...].astype(bf16))
    qn = _dot(qb, n_st[...].astype(bf16))
    lane_head = lax.broadcasted_iota(jnp.int32, (1, W), 1) // M_DH
    intra = jnp.zeros((L, W), f32)
    w_full = jnp.zeros((L, W), f32)
    r_full = jnp.zeros((L, W), f32)
    for h in range(M_HEADS):
        hm = lane_head == h
        dmat = jnp.where(causal, b_c[:, h:h + 1] - b_r[h:h + 1, :] + i_r[h:h + 1, :], -jnp.inf)
        a_h = a_c[:, h:h + 1]
        mt = jnp.maximum(a_h, jnp.max(dmat, -1, keepdims=True))
        pm = jnp.exp(dmat - mt)
        s = _nt(jnp.where(hm, qb, jnp.zeros_like(qb)), kb) * pm
        w_h = jnp.exp(a_h - mt)
        den = w_h * qn[:, h:h + 1] + jnp.sum(s, -1, keepdims=True)
        r_h = 1.0 / jnp.maximum(jnp.abs(den), jnp.exp(-mt))
        intra = intra + _dot(s.astype(bf16), jnp.where(hm, vb, jnp.zeros_like(vb)))
        w_full = jnp.where(hm, w_h, w_full)
        r_full = jnp.where(hm, r_h, r_full)
    hout = (w_full * inter + intra) * r_full
    ym_ref[0] = (hout * jax.nn.sigmoid(o_raw)).astype(ym_ref.dtype)

    b_end = b_r[:, L - 1:L]
    g_r = b_end - b_r + i_r
    mc = m_col[...][:, :1]
    m_new = jnp.maximum(b_end + mc, jnp.max(g_r, -1, keepdims=True))
    w_old = jnp.exp(b_end + mc - m_new)
    w_tok = jnp.exp(g_r - m_new)
    w_tok_full = jnp.concatenate([jnp.broadcast_to(w_tok[h:h + 1, :], (M_DH, L)) for h in range(M_HEADS)], axis=0)
    w_old_full = jnp.concatenate([jnp.broadcast_to(w_old[h:h + 1, :], (M_DH, 1)) for h in range(M_HEADS)], axis=0)
    ktw = (kt * w_tok_full).astype(bf16)
    row_head = lax.broadcasted_iota(jnp.int32, (W, W), 0) // M_DH
    col_head = lax.broadcasted_iota(jnp.int32, (W, W), 1) // M_DH
    c_st[...] = w_old_full * c_st[...] + jnp.where(row_head == col_head, _dot(ktw, vb), 0.0)
    n_rows = lax.broadcasted_iota(jnp.int32, (W, LANES), 0) // M_DH
    n_cols = lax.broadcasted_iota(jnp.int32, (W, LANES), 1)
    ksum = _dot(ktw, jnp.ones((L, LANES), bf16))
    n_st[...] = w_old_full * n_st[...] + jnp.where(n_rows == n_cols, ksum, 0.0)
    eye = (lax.broadcasted_iota(jnp.int32, (SUBLANES, LANES), 0)
           == lax.broadcasted_iota(jnp.int32, (SUBLANES, LANES), 1)).astype(f32)
    m_row[...] = jnp.sum(eye * m_new, axis=0, keepdims=True)
    m_col[...] = jnp.broadcast_to(m_new, (SUBLANES, LANES))

    @pl.when(c == pl.num_programs(1) - 1)
    def _():
        c_out_ref[0] = c_st[...]
        n_out_ref[0] = n_st[...]
        m_out_ref[0] = m_col[...]


def mlstm_prompt(zm, cw, cb, wq, wk, wkt, wv, bi, bfg, L):
    B, S, _ = zm.shape
    W = M_WIDTH
    full = lambda a: pl.BlockSpec(a.shape, lambda b, c: (0,) * a.ndim)
    return pl.pallas_call(
        _mlstm_kernel,
        grid=(B, S // L),
        in_specs=[pl.BlockSpec((1, L, ZM_W), lambda b, c: (b, c, 0)),
                  full(cw), full(cb), full(wq), full(wk), full(wkt), full(wv), full(bi), full(bfg)],
        out_specs=[pl.BlockSpec((1, L, W), lambda b, c: (b, c, 0)),
                   pl.BlockSpec((1, W, W), lambda b, c: (b, 0, 0)),
                   pl.BlockSpec((1, W, LANES), lambda b, c: (b, 0, 0)),
                   pl.BlockSpec((1, SUBLANES, LANES), lambda b, c: (b, 0, 0))],
        out_shape=[jax.ShapeDtypeStruct((B, S, W), bf16),
                   jax.ShapeDtypeStruct((B, W, W), f32),
                   jax.ShapeDtypeStruct((B, W, LANES), f32),
                   jax.ShapeDtypeStruct((B, SUBLANES, LANES), f32)],
        scratch_shapes=[pltpu.VMEM((W, W), f32), pltpu.VMEM((W, LANES), f32),
                        pltpu.VMEM((1, LANES), f32), pltpu.VMEM((SUBLANES, LANES), f32),
                        pltpu.VMEM((SUBLANES, W), f32)],
        compiler_params=_params(("arbitrary", "arbitrary")),
        name="mlstm_prompt",
    )(zm, cw, cb, wq, wk, wkt, wv, bi, bfg)


MOBA_VH = C_DH + 16


def _max_rows(x):
    r = x.shape[0]
    while r > SUBLANES:
        r //= 2
        x = jnp.maximum(x[:r], x[r:])
    return jnp.max(x, axis=0, keepdims=True)


def _moba_prompt_kernel(tab_ref, q_ref, k_ref, v_ref, o_ref,
                        k16, vt16, kmean, b_own, b_prev, sel_ref, qm_ref, acc_ref, s_ref, p_ref):
    S = k_ref.shape[1]
    nb = S // C_BLOCK
    T = C_BLOCK
    H = C_HEADS
    i = pl.program_id(1)
    key_i = lax.broadcasted_iota(jnp.int32, (T, T), 0)
    qry_i = lax.broadcasted_iota(jnp.int32, (T, T), 1)

    @pl.when(i == 0)
    def _():
        kf = k_ref[0]
        ones = jnp.ones((MOBA_VH - C_DH, T), bf16)
        for j in range(nb):
            k16[j] = kf[j * T:(j + 1) * T, :].astype(bf16)
            vt = v_ref[0, j * T:(j + 1) * T, :].T.astype(bf16)
            vt16[j] = jnp.concatenate([x for h in range(H) for x in (vt[h * C_DH:(h + 1) * C_DH], ones)], axis=0)
        kmean[...] = jnp.sum(kf.reshape(nb, T, C_WIDTH), axis=1) * (1.0 / T)
        for h in range(H):
            b_own[h] = _rel_bias(qry_i - key_i, tab_ref, h)
            b_prev[h] = _rel_bias(qry_i - key_i + T, tab_ref, h)

    qf = q_ref[0]
    lane_head = lax.broadcasted_iota(jnp.int32, (1, C_WIDTH), 1) // C_DH
    jio = lax.broadcasted_iota(jnp.int32, (nb, T), 0)
    valid = jio < i
    scale = C_DH ** -0.5
    for h in range(H):
        hm = lane_head == h
        qm_ref[h] = (jnp.where(hm, qf, 0.0) * scale).astype(bf16)
        gs = _nt(jnp.where(hm, kmean[...], 0.0), qf, HI)
        sel = jnp.zeros((nb, T), f32)
        for j in range(nb):
            vj = gs[j:j + 1, :]
            beats = jnp.where(gs > vj, 1.0, jnp.where((gs == vj) & (jio < j), 1.0, 0.0))
            cnt = jnp.sum(jnp.where(valid, beats, 0.0), axis=0, keepdims=True)
            sel = jnp.where(jio == j, jnp.where(cnt < C_TOPK, 1.0, 0.0), sel)
        sel_ref[h] = (sel - 1.0) * (-NEG)

    rows = lambda h: slice(h * MOBA_VH, (h + 1) * MOBA_VH)

    kblk = k16[i]
    for h in range(H):
        s_ref[h] = _nt(kblk, qm_ref[h])
    ms = []
    for h in range(H):
        s = jnp.where(key_i <= qry_i, s_ref[h] + b_own[h], NEG)
        m0 = _max_rows(s)
        p_ref[h] = jnp.exp(s - m0).astype(bf16)
        ms.append(m0)
    for h in range(H):
        acc_ref[rows(h), :] = _dot(vt16[i, rows(h), :], p_ref[h])

    def past_block(j, bias_of, ms):
        kblk = k16[j]
        for h in range(H):
            s_ref[h] = _nt(kblk, qm_ref[h])
        ms_new, alphas = [], []
        for h in range(H):
            sj = s_ref[h] + (bias_of(h) + sel_ref[h, pl.ds(j, 1), :])
            m_new = jnp.maximum(ms[h], _max_rows(sj))
            alphas.append(jnp.exp(ms[h] - m_new))
            p_ref[h] = jnp.exp(sj - m_new).astype(bf16)
            ms_new.append(m_new)
        for h in range(H):
            acc_ref[rows(h), :] = alphas[h] * acc_ref[rows(h), :] + _dot(vt16[j, rows(h), :], p_ref[h])
        return tuple(ms_new)

    ms = tuple(ms)
    ms = lax.cond(i >= 1, lambda c: past_block(i - 1, lambda h: b_prev[h], c), lambda c: c, ms)
    lax.fori_loop(0, jnp.maximum(i - 1, 0),
                  lambda j, c: past_block(j, lambda h: tab_ref[h * REL_BUCKETS + REL_BUCKETS - 1], c), ms)
    res = jnp.concatenate(
        [acc_ref[h * MOBA_VH:h * MOBA_VH + C_DH, :] / acc_ref[h * MOBA_VH + C_DH:h * MOBA_VH + C_DH + 1, :]
         for h in range(H)], axis=0)
    o_ref[0] = res.T.astype(o_ref.dtype)


def moba_prompt(q, k, v, tab):
    B, S, W = q.shape
    nb = S // C_BLOCK
    T = C_BLOCK
    return pl.pallas_call(
        _moba_prompt_kernel,
        grid=(B, nb),
        in_specs=[pl.BlockSpec(memory_space=pltpu.SMEM),
                  pl.BlockSpec((1, T, W), lambda b, i: (b, i, 0)),
                  pl.BlockSpec((1, S, W), lambda b, i: (b, 0, 0)),
                  pl.BlockSpec((1, S, W), lambda b, i: (b, 0, 0))],
        out_specs=pl.BlockSpec((1, T, W), lambda b, i: (b, i, 0)),
        out_shape=jax.ShapeDtypeStruct((B, S, W), bf16),
        scratch_shapes=[pltpu.VMEM((nb, T, W), bf16), pltpu.VMEM((nb, C_HEADS * MOBA_VH, T), bf16),
                        pltpu.VMEM((nb, W), f32),
                        pltpu.VMEM((C_HEADS, T, T), f32), pltpu.VMEM((C_HEADS, T, T), f32),
                        pltpu.VMEM((C_HEADS, nb, T), f32), pltpu.VMEM((C_HEADS, T, W), bf16),
                        pltpu.VMEM((C_HEADS * MOBA_VH, T), f32),
                        pltpu.VMEM((C_HEADS, T, T), f32), pltpu.VMEM((C_HEADS, T, T), bf16)],
        compiler_params=_params(("arbitrary", "arbitrary")),
        name="moba_prompt",
    )(tab, q, k, v)


def _layer_out_kernel(x_ref, ya_ref, ym_ref, yc_ref, sg_ref, g1_ref, sh2_ref, sc2_ref, g2_ref,
                      wa_ref, wm_ref, wc_ref, wo_ref, l1g_ref, l1b_ref, l2g_ref, l2b_ref,
                      wg_ref, wu_ref, wd_ref, o_ref, x1_ref, h2_ref, f_ref, *, alpha):
    kf = pl.program_id(2)
    d = D_MODEL

    @pl.when(kf == 0)
    def _():
        pa = _dot(ya_ref[0], wa_ref[...])
        pm = _dot(ym_ref[0], wm_ref[...])
        pc = _dot(yc_ref[0], wc_ref[...])
        pre = (sg_ref[0, :, :d].astype(f32) * pa + sg_ref[0, :, d:2 * d].astype(f32) * pm
               + sg_ref[0, :, 2 * d:].astype(f32) * pc)
        mix = _dot(pre.astype(wo_ref.dtype), wo_ref[...])
        x1 = _layer_norm(alpha * x_ref[0] + g1_ref[0] * mix, l1g_ref[...], l1b_ref[...])
        x1_ref[...] = x1
        h2_ref[...] = (x1 * (1.0 + sc2_ref[0]) + sh2_ref[0]).astype(h2_ref.dtype)
        f_ref[...] = jnp.zeros_like(f_ref)

    h2 = h2_ref[...]
    t = _silu(_dot(h2, wg_ref[...])) * _dot(h2, wu_ref[...])
    f_ref[...] += _dot(t.astype(wd_ref.dtype), wd_ref[...])

    @pl.when(kf == pl.num_programs(2) - 1)
    def _():
        o_ref[0] = _layer_norm(alpha * x1_ref[...] + g2_ref[0] * f_ref[...], l2g_ref[...], l2b_ref[...])


def layer_out(x, ya, ym, yc, sg, mods, wa, wm, wc, wo, l1g, l1b, l2g, l2b, wg, wu, wd, tm, tf, alpha):
    B, S, d = x.shape
    dff = wg.shape[1]
    tok = lambda a: pl.BlockSpec((1, tm, a.shape[2]), lambda b, i, k: (b, i, 0))
    full = lambda a: pl.BlockSpec(a.shape, lambda b, i, k: (0,) * a.ndim, pipeline_mode=pl.Buffered(1))
    return pl.pallas_call(
        functools.partial(_layer_out_kernel, alpha=alpha),
        grid=(B, S // tm, dff // tf),
        in_specs=[tok(x), tok(ya), tok(ym), tok(yc), tok(sg),
                  _mod_spec(mods, tm, 2), _mod_spec(mods, tm, 3), _mod_spec(mods, tm, 4), _mod_spec(mods, tm, 5),
                  full(wa), full(wm), full(wc), full(wo), full(l1g), full(l1b), full(l2g), full(l2b),
                  pl.BlockSpec((d, tf), lambda b, i, k: (0, k)),
                  pl.BlockSpec((d, tf), lambda b, i, k: (0, k)),
                  pl.BlockSpec((tf, d), lambda b, i, k: (k, 0))],
        out_specs=pl.BlockSpec((1, tm, d), lambda b, i, k: (b, i, 0)),
        out_shape=jax.ShapeDtypeStruct((B, S, d), f32),
        scratch_shapes=[pltpu.VMEM((tm, d), f32), pltpu.VMEM((tm, d), bf16), pltpu.VMEM((tm, d), f32)],
        compiler_params=_params(("arbitrary", "arbitrary", "arbitrary")),
        name="layer_out",
    )(x, ya, ym, yc, sg, mods, mods, mods, mods, wa, wm, wc, wo, l1g, l1b, l2g, l2b, wg, wu, wd)


def _absorb_kernel(qh_ref, wabs_ref, o_ref):
    for h in range(A_HEADS):
        blk = slice(h * HEAD_BLK, (h + 1) * HEAD_BLK)
        o_ref[:, blk] = _dot(qh_ref[:, blk], wabs_ref[h], HI)


def mla_absorb(qh, wabs):
    return pl.pallas_call(_absorb_kernel, out_shape=jax.ShapeDtypeStruct(qh.shape, f32),
                          compiler_params=_params(None), name="mla_absorb")(qh, wabs)


def _paged_kernel(pt_ref, qa_ref, qb_ref, latn_ref, krn_ref, qcol_ref, lat_hbm, krt_hbm, kt_hbm,
                  olat_ref, idx_ref, lat_buf, kr_buf, kt_buf, sems, m_s, l_s, acc_s, g_s,
                  *, layer, pages, cps):
    g = pl.program_id(0)
    c = g % cps
    slot = g % 2
    ppb = C_BLOCK // PAGE_SIZE
    blocks = pages // ppb

    def copies(step, sl):
        b = step // cps
        first = (step % cps) * pages
        out = []
        for j in range(pages):
            pg = pt_ref[b, first + j]
            out.append(pltpu.make_async_copy(lat_hbm.at[layer, pg], lat_buf.at[sl, j], sems.at[sl, 0]))
            out.append(pltpu.make_async_copy(krt_hbm.at[layer, pg], kr_buf.at[sl, j], sems.at[sl, 1]))
            out.append(pltpu.make_async_copy(kt_hbm.at[layer, pg], kt_buf.at[sl, j], sems.at[sl, 2]))
        return out

    @pl.when(g == 0)
    def _():
        for cp in copies(0, 0):
            cp.start()

    @pl.when(g + 1 < pl.num_programs(0))
    def _():
        for cp in copies(g + 1, 1 - slot):
            cp.start()

    qa = qa_ref[0]
    qr = qb_ref[0][:, A_NOPE:A_NOPE + A_ROPE]

    @pl.when(c == 0)
    def _():
        latn = latn_ref[0]
        s0 = jnp.sum(qa * latn, -1, keepdims=True) + jnp.sum(qr * krn_ref[0], -1, keepdims=True)
        m_s[...] = s0
        l_s[...] = jnp.ones_like(l_s)
        acc_s[...] = jnp.broadcast_to(latn, acc_s.shape)

    for cp in copies(g, slot):
        cp.wait()

    lat = lat_buf[slot].reshape(pages * PAGE_SIZE, A_KV_LORA).astype(bf16)
    krt = jnp.concatenate([kr_buf[slot, j] for j in range(pages)], axis=1).astype(bf16)
    s = _nt(qa.astype(bf16), lat) + _dot(qr.astype(bf16), krt)
    m_old = m_s[...]
    m_new = jnp.maximum(m_old, jnp.max(s, -1, keepdims=True))
    alpha = jnp.exp(m_old - m_new)
    p = jnp.exp(s - m_new)
    l_s[...] = alpha * l_s[...] + jnp.sum(p, -1, keepdims=True)
    acc_s[...] = alpha * acc_s[...] + _dot(p.astype(bf16), lat)
    m_s[...] = m_new

    qcb = jnp.broadcast_to(qcol_ref[0], (C_WIDTH, PAGE_SIZE))
    for jj in range(blocks):
        ks = kt_buf[slot, jj * ppb]
        for t in range(1, ppb):
            ks = ks + kt_buf[slot, jj * ppb + t]
        w = ks * qcb
        for h in range(C_HEADS):
            row = jnp.sum(w[h * C_DH:(h + 1) * C_DH, :], axis=0, keepdims=True)
            g_s[pl.ds(h * (cps * blocks) + c * blocks + jj, 1), :] = row

    @pl.when(c == cps - 1)
    def _():
        olat_ref[0] = acc_s[...] / l_s[...]
        nbp = cps * blocks
        gsum = _nt(jnp.ones((SUBLANES, PAGE_SIZE), f32), g_s[...], HI) * (1.0 / C_BLOCK)
        row = lax.broadcasted_iota(jnp.int32, gsum.shape, 0)
        lane_i = lax.broadcasted_iota(jnp.int32, gsum.shape, 1)
        lane = lane_i.astype(f32)
        gm = jnp.where(lane_i // nbp == row, gsum, -jnp.inf)
        out_lane = lax.broadcasted_iota(jnp.int32, (SUBLANES, LANES), 1)
        out = jnp.zeros((SUBLANES, LANES), f32)
        for r in range(C_TOPK):
            mx = jnp.max(gm, -1, keepdims=True)
            ix = jnp.min(jnp.where(gm == mx, lane, float(gm.shape[1])), -1, keepdims=True)
            out = jnp.where(out_lane == r, ix, out)
            gm = jnp.where(lane == ix, -jnp.inf, gm)
        head_off = (lax.broadcasted_iota(jnp.int32, (SUBLANES, LANES), 0) * nbp).astype(f32)
        idx_ref[0] = (out - head_off).astype(jnp.int32)


def sample_paged(page_table, qa, qb, latn, krn, qcol, cache_lat, cache_krt, cache_kt, layer, pages):
    DB, n_pages = page_table.shape
    cps = n_pages // pages
    nbp = n_pages * PAGE_SIZE // C_BLOCK
    seq = lambda a: pl.BlockSpec((1,) + a.shape[1:], lambda g, pt: (g // cps, 0, 0))
    hbm = pl.BlockSpec(memory_space=pl.ANY)
    grid_spec = pltpu.PrefetchScalarGridSpec(
        num_scalar_prefetch=1,
        grid=(DB * cps,),
        in_specs=[seq(qa), seq(qb), seq(latn), seq(krn), seq(qcol), hbm, hbm, hbm],
        out_specs=[pl.BlockSpec((1, A_HEADS, A_KV_LORA), lambda g, pt: (g // cps, 0, 0)),
                   pl.BlockSpec((1, SUBLANES, LANES), lambda g, pt: (g // cps, 0, 0))],
        scratch_shapes=[pltpu.VMEM((2, pages, PAGE_SIZE, A_KV_LORA), f32),
                        pltpu.VMEM((2, pages, A_ROPE, PAGE_SIZE), f32),
                        pltpu.VMEM((2, pages, C_WIDTH, PAGE_SIZE), f32),
                        pltpu.SemaphoreType.DMA((2, 3)),
                        pltpu.VMEM((A_HEADS, 1), f32), pltpu.VMEM((A_HEADS, 1), f32),
                        pltpu.VMEM((A_HEADS, A_KV_LORA), f32),
                        pltpu.VMEM((C_HEADS * nbp, PAGE_SIZE), f32)],
    )
    return pl.pallas_call(
        functools.partial(_paged_kernel, layer=layer, pages=pages, cps=cps),
        grid_spec=grid_spec,
        out_shape=[jax.ShapeDtypeStruct((DB, A_HEADS, A_KV_LORA), f32),
                   jax.ShapeDtypeStruct((DB, SUBLANES, LANES), jnp.int32)],
        compiler_params=_params(("arbitrary",)),
        name="sample_paged",
    )(page_table, qa, qb, latn, krn, qcol, cache_lat, cache_krt, cache_kt)


def _value_up_kernel(o_ref, w_ref, y_ref):
    y_ref[...] = _dot(o_ref[...], w_ref[...], HI).astype(y_ref.dtype)


def mla_value_up(olat, wuv_bd):
    return pl.pallas_call(_value_up_kernel,
                          out_shape=jax.ShapeDtypeStruct((olat.shape[0], wuv_bd.shape[1]), bf16),
                          compiler_params=_params(None), name="mla_value_up")(olat, wuv_bd)


def _mlstm_step_kernel(zm_ref, cs_ref, cw_ref, cb_ref, wqt_ref, wkt_ref, wvt_ref, bi_ref, bf_ref,
                       c0_ref, n0_ref, m0_ref, h_ref, c1_ref, n1_ref, m1_ref, qt_s, kt_s):
    W = M_WIDTH
    zm = zm_ref[...]
    u = zm[:, :W]
    acc = u * cw_ref[M_CONV - 1:M_CONV, :]
    for j in range(M_CONV - 1):
        acc = acc + cs_ref[j] * cw_ref[j:j + 1, :]
    uc = _silu(acc + cb_ref[...])
    qt_s[...] = _nt(wqt_ref[...], uc, HI)
    kt_s[...] = _nt(wkt_ref[...], uc, HI)
    vt = _nt(wvt_ref[...], u, HI)
    gate_t = jax.nn.sigmoid(zm[:, W:2 * W]).T
    i_t = (zm[:, 2 * W:2 * W + LANES] + bi_ref[...]).T[:SUBLANES, :]
    f_t = _log_sigmoid(zm[:, 2 * W + LANES:] + bf_ref[...]).T[:SUBLANES, :]
    outs = []
    for h in range(M_HEADS):
        rows = slice(h * M_DH, (h + 1) * M_DH)
        i_h = i_t[h:h + 1, :]
        a = f_t[h:h + 1, :] + m0_ref[h:h + 1, :]
        mt = jnp.maximum(a, i_h)
        w_old = jnp.exp(a - mt)
        w_tok = jnp.exp(i_h - mt)
        qh = qt_s[rows, :]
        kh = kt_s[rows, :]
        vh = vt[rows, :]
        s = jnp.sum(qh * kh, axis=0, keepdims=True) * w_tok
        qn = jnp.sum(qh * n0_ref[h], axis=0, keepdims=True)

        def body(d, qc, h=h, w_old=w_old, w_tok=w_tok, vh=vh):
            c0 = c0_ref[h, d]
            k_row = kt_s[pl.ds(h * M_DH + d, 1), :]
            q_row = qt_s[pl.ds(h * M_DH + d, 1), :]
            c1_ref[h, d] = w_old * c0 + (w_tok * k_row) * vh
            return qc + q_row * c0

        qc = lax.fori_loop(0, M_DH, body, jnp.zeros((M_DH, zm.shape[0]), f32))
        num = w_old * qc + s * vh
        den = w_old * qn + s
        outs.append(num / jnp.maximum(jnp.abs(den), jnp.exp(-mt)) * gate_t[rows, :])
        n1_ref[h] = w_old * n0_ref[h] + w_tok * kh
        m1_ref[h:h + 1, :] = mt
    h_ref[...] = jnp.concatenate(outs, axis=0).T.astype(h_ref.dtype)


def mlstm_sample_step(zm, conv_t, cw, cb, wqt, wkt, wvt, bi, bfg, c0, n0, m0):
    DB = zm.shape[0]
    shapes = [jax.ShapeDtypeStruct((DB, M_WIDTH), bf16), jax.ShapeDtypeStruct(c0.shape, f32),
              jax.ShapeDtypeStruct(n0.shape, f32), jax.ShapeDtypeStruct(m0.shape, f32)]
    return pl.pallas_call(
        _mlstm_step_kernel, out_shape=shapes,
        scratch_shapes=[pltpu.VMEM((M_WIDTH, DB), f32), pltpu.VMEM((M_WIDTH, DB), f32)],
        compiler_params=_params(None), name="mlstm_sample_step",
    )(zm, conv_t, cw, cb, wqt, wkt, wvt, bi, bfg, c0, n0, m0)


def _moba_decode_kernel(pt_ref, idx_ref, tab_ref, q_ref, kn_ref, vn_ref, kt_hbm, vt_hbm, o_ref,
                        k_buf, v_buf, sems, *, layer, n_sel, past):
    b = pl.program_id(0)
    slot = b % 2
    ppb = C_BLOCK // PAGE_SIZE
    scale = C_DH ** -0.5

    def copies(seq, sl):
        out = []
        for h in range(C_HEADS):
            for j in range(n_sel):
                blk = idx_ref[(seq * C_HEADS + h) * C_TOPK + j // ppb]
                pg = pt_ref[seq, blk * ppb + j % ppb]
                rows = pl.ds(h * C_DH, C_DH)
                out.append(pltpu.make_async_copy(kt_hbm.at[layer, pg, rows], k_buf.at[sl, h * n_sel + j], sems.at[sl, 0]))
                out.append(pltpu.make_async_copy(vt_hbm.at[layer, pg, rows], v_buf.at[sl, h * n_sel + j], sems.at[sl, 1]))
        return out

    @pl.when(b == 0)
    def _():
        for cp in copies(0, 0):
            cp.start()

    @pl.when(b + 1 < pl.num_programs(0))
    def _():
        for cp in copies(b + 1, 1 - slot):
            cp.start()

    for cp in copies(b, slot):
        cp.wait()

    t = lax.broadcasted_iota(jnp.int32, (1, C_BLOCK), 1)
    heads = range(C_HEADS)
    scores = []
    for h in heads:
        kt = jnp.concatenate([k_buf[slot, h * n_sel + j] for j in range(n_sel)], axis=1).astype(bf16)
        q8 = jnp.broadcast_to((q_ref[h] * scale).astype(bf16), (SUBLANES, C_DH))
        scores.append(_dot(q8, kt)[:1, :])
    probs, p_selfs, ls = [], [], []
    for h in heads:
        q = q_ref[h]
        bias = [_rel_bias(past - (idx_ref[(b * C_HEADS + h) * C_TOPK + j] * C_BLOCK + t), tab_ref, h)
                for j in range(n_sel // ppb)]
        s = scores[h] + jnp.concatenate(bias, axis=1)
        s_self = jnp.sum(q * kn_ref[h], -1, keepdims=True) * scale + tab_ref[h * REL_BUCKETS]
        m = jnp.maximum(jnp.max(s, -1, keepdims=True), s_self)
        p = jnp.exp(s - m)
        p_self = jnp.exp(s_self - m)
        ls.append(jnp.sum(p, -1, keepdims=True) + p_self)
        p_selfs.append(p_self)
        probs.append(jnp.broadcast_to(p.astype(bf16), (SUBLANES, p.shape[1])))
    for h in heads:
        vt = jnp.concatenate([v_buf[slot, h * n_sel + j] for j in range(n_sel)], axis=1).astype(bf16)
        pv = _nt(probs[h], vt)[:1, :]
        o_ref[h] = (pv + p_selfs[h] * vn_ref[h]) / ls[h]


def moba_decode(page_table, idx_flat, tab, q, kn, vn, cache_kt, cache_vt, layer):
    DB, n_pages = page_table.shape
    ppb = C_BLOCK // PAGE_SIZE
    n_sel = C_TOPK * ppb
    past = n_pages * PAGE_SIZE
    seq = pl.BlockSpec((None, C_HEADS, 1, C_DH), lambda b, pt, idx: (b, 0, 0, 0))
    hbm = pl.BlockSpec(memory_space=pl.ANY)
    grid_spec = pltpu.PrefetchScalarGridSpec(
        num_scalar_prefetch=2,
        grid=(DB,),
        in_specs=[pl.BlockSpec(memory_space=pltpu.SMEM), seq, seq, seq, hbm, hbm],
        out_specs=seq,
        scratch_shapes=[pltpu.VMEM((2, C_HEADS * n_sel, C_DH, PAGE_SIZE), f32),
                        pltpu.VMEM((2, C_HEADS * n_sel, C_DH, PAGE_SIZE), f32),
                        pltpu.SemaphoreType.DMA((2, 2))],
    )
    return pl.pallas_call(
        functools.partial(_moba_decode_kernel, layer=layer, n_sel=n_sel, past=past),
        grid_spec=grid_spec,
        out_shape=jax.ShapeDtypeStruct((DB, C_HEADS, 1, C_DH), f32),
        compiler_params=_params(("arbitrary",)),
        name="moba_decode",
    )(page_table, idx_flat, tab, q, kn, vn, cache_kt, cache_vt)


def _pack_w_in(w):
    d = w.shape[0]
    z = lambda n: jnp.zeros((d, n), w.dtype)
    o = 0
    parts = {}
    for name, size in (("cq", A_Q_LORA), ("ckv", A_KV_LORA), ("kr", A_ROPE), ("u", M_WIDTH), ("o", M_WIDTH),
                       ("i", M_HEADS), ("f", M_HEADS), ("qkv", 3 * C_WIDTH), ("g", 3 * D_MODEL)):
        parts[name] = w[:, o:o + size]
        o += size
    kr = parts["kr"]
    half = A_ROPE // 2
    pad = LANES - A_NOPE - A_ROPE
    krp = jnp.concatenate([z(A_NOPE), kr, z(pad)], axis=1)
    krs = jnp.concatenate([z(A_NOPE), kr[:, half:], kr[:, :half], z(pad)], axis=1)
    return jnp.concatenate([parts["cq"], parts["ckv"], krp, krs, parts["u"], parts["o"],
                            parts["i"], z(LANES - M_HEADS), parts["f"], z(LANES - M_HEADS),
                            parts["qkv"], parts["g"]], axis=1)


def _pack_w_q(w):
    r = w.shape[0]
    w3 = w.reshape(r, A_HEADS, A_NOPE + A_ROPE)
    half = A_ROPE // 2
    pad = jnp.zeros((r, A_HEADS, LANES - A_NOPE - A_ROPE), w.dtype)
    nope, rope = w3[..., :A_NOPE], w3[..., A_NOPE:]
    w1 = jnp.concatenate([nope, rope, pad], axis=-1)
    w2 = jnp.concatenate([jnp.zeros_like(nope), rope[..., half:], rope[..., :half], pad], axis=-1)
    return w1.reshape(r, -1), w2.reshape(r, -1)


def _rope_tables(pos):
    inv = ROPE_THETA ** (-jnp.arange(0, A_ROPE, 2, dtype=f32) / A_ROPE)
    ang = pos.astype(f32)[:, None] * inv[None, :]
    cos, sin = jnp.cos(ang), jnp.sin(ang)
    n = pos.shape[0]
    pad = jnp.zeros((n, LANES - A_NOPE - A_ROPE), f32)
    ct = jnp.concatenate([jnp.ones((n, A_NOPE), f32), cos, cos, pad], axis=1)
    st = jnp.concatenate([jnp.zeros((n, A_NOPE), f32), -sin, sin, pad], axis=1)
    return ct, st


def _block_diag(w):
    hN, a, b = w.shape
    eye = jnp.eye(hN, dtype=w.dtype)
    return (eye[:, None, :, None] * w[:, :, None, :]).reshape(hN * a, hN * b)


def _lane_pad(v, width=LANES):
    return jnp.pad(v, ((0, 0), (0, width - v.shape[-1])))


def kernel(x_prompt, x_sample, cache_mla_latent, cache_mla_krope, cache_moba_k, cache_moba_v, state_mlstm_C, state_mlstm_n, state_mlstm_m, state_mlstm_conv, page_table, c_prompt, c_sample, rel_table, w_ada, b_ada, w_in, g_q_norm, w_q_up, g_kv_norm, w_kv_up, conv_w, conv_b, w_mq, w_mk, w_mv, b_i, b_f, w_br_a, w_br_m, w_br_c, w_out, ln1_g, ln1_b, w_ff_gate, w_ff_up, w_ff_down, ln2_g, ln2_b):
    B, S, d = x_prompt.shape
    DB, T, _ = x_sample.shape
    depth = w_in.shape[0]
    n_pool = cache_mla_latent.shape[1]
    n_pages = page_table.shape[1]
    past = n_pages * PAGE_SIZE
    assert T == 1 and past % C_BLOCK == 0 and past // C_BLOCK >= C_TOPK and S % C_BLOCK == 0
    alpha = (2 * depth) ** 0.25
    dff = w_ff_gate.shape[2]

    mods_all = ada_mods(jnp.concatenate([c_prompt, c_sample], axis=0), w_ada, b_ada)
    ct_p, st_p = _rope_tables(jnp.arange(S, dtype=jnp.int32))
    ct_s, st_s = _rope_tables(jnp.full((DB,), past, jnp.int32))
    tab = rel_table.T.reshape(-1)
    cache_kt = jnp.transpose(cache_moba_k, (0, 1, 3, 4, 2)).reshape(depth, n_pool, C_WIDTH, PAGE_SIZE)
    cache_vt = jnp.transpose(cache_moba_v, (0, 1, 3, 4, 2)).reshape(depth, n_pool, C_WIDTH, PAGE_SIZE)
    cache_krt = jnp.transpose(cache_mla_krope, (0, 1, 3, 2))
    xs = x_sample.reshape(1, DB, d)
    xp = x_prompt

    new_p = [[] for _ in range(8)]
    new_s = [[] for _ in range(8)]
    for l in range(depth):
        w_pack = _pack_w_in(w_in[l]).astype(bf16)
        wq1, wq2 = _pack_w_q(w_q_up[l])
        w_uk = w_kv_up[l][..., :A_NOPE]
        w_uv = w_kv_up[l][..., A_NOPE:]
        wuk = jnp.concatenate([w_uk, jnp.zeros_like(w_uk)], axis=-1).reshape(A_KV_LORA, -1)
        wuv = w_uv.reshape(A_KV_LORA, -1)
        gq = g_q_norm[l][None, :]
        gkv = g_kv_norm[l][None, :]
        wq_bd = _block_diag(w_mq[l])
        wk_bd = _block_diag(w_mk[l]) * (M_DH ** -0.5)
        wv_bd = _block_diag(w_mv[l])
        bi = _lane_pad(b_i[l][None, :])
        bfg = _lane_pad(b_f[l][None, :])
        cw = conv_w[l]
        cb = conv_b[l][None, :]
        tail_w = (w_br_a[l].astype(bf16), w_br_m[l].astype(bf16), w_br_c[l].astype(bf16), w_out[l].astype(bf16),
                  ln1_g[l][None, :], ln1_b[l][None, :], ln2_g[l][None, :], ln2_b[l][None, :],
                  w_ff_gate[l].astype(bf16), w_ff_up[l].astype(bf16), w_ff_down[l].astype(bf16))

        mods_p = mods_all[l, :B].reshape(B, 1, 6 * d)
        za, zm, qc, kc, vc, sg = in_proj(xp, mods_p, w_pack, tm=512)
        qh, kh, vh, lat, krope = mla_prep(za, ct_p, st_p, gq, gkv, wq1.astype(bf16), wq2.astype(bf16),
                                          wuk.astype(bf16), wuv.astype(bf16), tm=512, qk_dtype=bf16)
        ya = mla_attn(qh, kh, vh, tq=512)
        ym, c_st, n_st, m_st = mlstm_prompt(zm, cw, cb, wq_bd.astype(bf16), wk_bd.astype(bf16),
                                            wk_bd.T.astype(bf16), wv_bd.astype(bf16), bi, bfg, L=256)
        yc = moba_prompt(qc, kc, vc, tab)
        xp_new = layer_out(xp, ya, ym, yc, sg, mods_p, *tail_w, tm=256, tf=dff, alpha=alpha)
        C1 = jnp.stack([c_st[:, h * M_DH:(h + 1) * M_DH, h * M_DH:(h + 1) * M_DH] for h in range(M_HEADS)], axis=1)
        n1 = jnp.stack([n_st[:, h * M_DH:(h + 1) * M_DH, h] for h in range(M_HEADS)], axis=1)
        m1 = m_st[:, :M_HEADS, 0]
        u_p = zm[:, :, :M_WIDTH]
        conv_p = jnp.pad(u_p, ((0, 0), (M_CONV - 1, 0), (0, 0)))[:, -(M_CONV - 1):]
        for lst, val in zip(new_p, (lat, krope, kc.reshape(B, S, C_HEADS, C_DH), vc.reshape(B, S, C_HEADS, C_DH),
                                    C1, n1, m1, conv_p)):
            lst.append(val)
        xp = xp_new

        mods_s = mods_all[l, B:].reshape(1, DB, 6 * d)
        za, zm, qc, kc, vc, sg = in_proj(xs, mods_s, w_pack, tm=DB)
        qh, _, _, lat, krope = mla_prep(za, ct_s, st_s, gq, gkv, wq1, wq2, wuk, wuv, tm=DB, qk_dtype=f32)
        wabs = jnp.concatenate([jnp.transpose(w_uk, (1, 2, 0)),
                                jnp.zeros((A_HEADS, HEAD_BLK - A_NOPE, A_KV_LORA), f32)], axis=1)
        qh2 = qh.reshape(DB, A_HEADS * HEAD_BLK)
        qabs = mla_absorb(qh2, wabs)
        olat, idx = sample_paged(page_table, qabs.reshape(DB, A_HEADS, A_KV_LORA), qh2.reshape(DB, A_HEADS, HEAD_BLK),
                                 lat.reshape(DB, 1, A_KV_LORA), krope.reshape(DB, 1, A_ROPE),
                                 qc.reshape(DB, C_WIDTH, 1), cache_mla_latent, cache_krt, cache_kt, l, pages=64)
        wuv_bd = _block_diag(jnp.transpose(w_uv, (1, 0, 2)))
        ya = mla_value_up(olat.reshape(DB, A_HEADS * A_KV_LORA), wuv_bd)

        zm2 = zm.reshape(DB, ZM_W)
        conv_state = state_mlstm_conv[l]
        ym, C1t, n1t, m1t = mlstm_sample_step(
            zm2, jnp.transpose(conv_state, (1, 0, 2)), cw, cb, wq_bd.T, wk_bd.T, wv_bd.T, bi, bfg,
            jnp.transpose(state_mlstm_C[l], (1, 2, 3, 0)), jnp.transpose(state_mlstm_n[l], (1, 2, 0)),
            jnp.transpose(state_mlstm_m[l], (1, 0)))
        C1 = jnp.transpose(C1t, (3, 0, 1, 2))
        n1 = jnp.transpose(n1t, (2, 0, 1))
        m1 = jnp.transpose(m1t, (1, 0))
        u_s = zm2[:, :M_WIDTH]
        conv_s = jnp.concatenate([conv_state[:, 1:], u_s[:, None, :]], axis=1)

        idx_flat = idx[:, :C_HEADS, :C_TOPK].reshape(-1)
        per_head = lambda a: a.reshape(DB, C_HEADS, 1, C_DH)
        yc = moba_decode(page_table, idx_flat, tab, per_head(qc), per_head(kc), per_head(vc), cache_kt, cache_vt, l)
        xs_new = layer_out(xs, ya.reshape(1, DB, -1), ym.reshape(1, DB, -1), yc.reshape(1, DB, -1).astype(bf16),
                           sg, mods_s, *tail_w, tm=DB, tf=dff // 2, alpha=alpha)
        for lst, val in zip(new_s, (lat.reshape(DB, 1, A_KV_LORA), krope.reshape(DB, 1, A_ROPE),
                                    kc.reshape(DB, 1, C_HEADS, C_DH), vc.reshape(DB, 1, C_HEADS, C_DH),
                                    C1.reshape(DB, M_HEADS, M_DH, M_DH), n1.reshape(DB, M_HEADS, M_DH),
                                    m1.reshape(DB, M_HEADS), conv_s)):
            lst.append(val)
        xs = xs_new

    outs_p = [jnp.stack(a) for a in new_p]
    outs_s = [jnp.stack(a) for a in new_s]
    return (xp, xs.reshape(DB, T, d), *outs_p, *outs_s)
```
